```python
import jax, jax.numpy as jnp
from jax import lax
import numpy as np

D_MODEL = 1024
BATCH = 4
SEQ = 4096
DEPTH = 2

GRID_W = 64
CTX_LEN = 256
HEAD_DIM = 64
NORM_EPS = 1e-6
ROPE_BASE = 10000.0

POOL_WIDTH = 512
POOL_GROUPS = 4
POOL_WINDOWS = (2, 4, 8, 16)
POOL_GROUP_W = POOL_WIDTH // POOL_GROUPS

NA_HEADS = 8
NA_KH = 8
NA_KW = 16
NA_QC = 16
NA_KC = 32

SWA_HEADS = 8
SWA_KV_HEADS = 2
SWA_WINDOW = 128
SWA_BLOCK = 128

PEER_HEADS = 8
PEER_NKEYS = 128
PEER_EXPERTS = PEER_NKEYS * PEER_NKEYS
PEER_DKEY = 256
PEER_TOPK = 16
PEER_CHUNK = 128

NA_WIDTH = NA_HEADS * HEAD_DIM
SWA_Q_WIDTH = SWA_HEADS * HEAD_DIM
SWA_KV_WIDTH = SWA_KV_HEADS * HEAD_DIM
OFF_KN = 0
OFF_VN = OFF_KN + NA_WIDTH
OFF_KW = OFF_VN + NA_WIDTH
OFF_VW = OFF_KW + SWA_KV_WIDTH
CTX_KV_COLS = OFF_VW + SWA_KV_WIDTH
OFF_QN = CTX_KV_COLS
OFF_QW = OFF_QN + NA_WIDTH
OFF_A = OFF_QW + SWA_Q_WIDTH
OFF_G = OFF_A + POOL_WIDTH
IN_COLS = OFF_G + 3 * D_MODEL

kernel_name = 'hybrid_pool_natten_swa_peer_dit'


def rms_norm(x, g):
    xf = x.astype(jnp.float32)
    y = xf * lax.rsqrt(jnp.mean(xf * xf, axis=-1, keepdims=True) + NORM_EPS)
    return (y * g.astype(jnp.float32)).astype(x.dtype)


def modulate(h, shift, scale):
    return h * (1 + scale) + shift


def heads(z, off, n):
    return z[..., off:off + n * HEAD_DIM].reshape(*z.shape[:-1], n, HEAD_DIM)


def rope_axis(x, pos):
    f = x.shape[-1] // 2
    inv = ROPE_BASE ** (-jnp.arange(f, dtype=jnp.float32) / f)
    ang = pos.astype(jnp.float32)[:, None] * inv[None, :]
    cos = jnp.cos(ang)[None, :, None, :]
    sin = jnp.sin(ang)[None, :, None, :]
    xf = x.astype(jnp.float32)
    x1, x2 = xf[..., :f], xf[..., f:]
    return jnp.concatenate([x1 * cos - x2 * sin, x1 * sin + x2 * cos], axis=-1).astype(x.dtype)


def rope_2d(x, row, col):
    half = x.shape[-1] // 2
    return jnp.concatenate([rope_axis(x[..., :half], row), rope_axis(x[..., half:], col)], axis=-1)


def centred_pool_minus_self(a, w):
    T = a.shape[1]
    af = a.astype(jnp.float32)
    csum = jnp.concatenate([jnp.zeros_like(af[:, :1]), jnp.cumsum(af, axis=1)], axis=1)
    t = jnp.arange(T)
    lo = jnp.clip(t - w // 2, 0, T)
    hi = jnp.clip(t - w // 2 + w, 0, T)
    mean = (csum[:, hi] - csum[:, lo]) / (hi - lo).astype(jnp.float32)[None, :, None]
    return (mean - af).astype(a.dtype)


def pool_mixer(a, w_pool, pool_scale):
    B, T, _ = a.shape
    ag = a.reshape(B, T, POOL_GROUPS, POOL_GROUP_W)
    pooled = jnp.stack([centred_pool_minus_self(ag[:, :, g], w) for g, w in enumerate(POOL_WINDOWS)], axis=2)
    y = jnp.einsum('btgc,gce->btge', pooled, w_pool).reshape(B, T, POOL_WIDTH)
    return y * pool_scale


def neighbourhood_attention(q, k, v, k_ctx, v_ctx, rpb):
    B, S, H, d = q.shape
    rows = S // GRID_W
    kh = min(NA_KH, rows)
    ncb = GRID_W // NA_QC
    r = jnp.arange(rows)
    row_idx = jnp.clip(r - kh // 2, 0, rows - kh)[:, None] + jnp.arange(kh)[None, :]
    cb = jnp.arange(ncb)
    kcol_idx = jnp.clip(cb * NA_QC - NA_KW // 2, 0, GRID_W - NA_KC)[:, None] + jnp.arange(NA_KC)[None, :]
    qcol = cb[:, None] * NA_QC + jnp.arange(NA_QC)[None, :]
    qcol_start = jnp.clip(qcol - NA_KW // 2, 0, GRID_W - NA_KW)
    col_ok = (kcol_idx[:, None, :] >= qcol_start[:, :, None]) & (kcol_idx[:, None, :] < qcol_start[:, :, None] + NA_KW)
    dr = row_idx - r[:, None] + NA_KH - 1
    dc = jnp.clip(kcol_idx[:, None, :] - qcol[:, :, None] + NA_KW - 1, 0, 2 * NA_KW - 2)
    bias = rpb.astype(jnp.float32)[:, dr[:, None, None, :, None], dc[None, :, :, None, :]]
    scale = d ** -0.5
    qg = q.reshape(B, rows, ncb, NA_QC, H, d)
    kgrid = k.reshape(B, rows, GRID_W, H, d)
    vgrid = v.reshape(B, rows, GRID_W, H, d)
    kg = kgrid[:, row_idx[:, :, None, None], kcol_idx[None, None, :, :]]
    vg = vgrid[:, row_idx[:, :, None, None], kcol_idx[None, None, :, :]]
    s_loc = jnp.einsum('brnqhd,brinjhd->bhrnqij', qg, kg).astype(jnp.float32) * scale + bias[None]
    s_loc = jnp.where(col_ok[:, :, None, :], s_loc, -jnp.inf)
    s_loc = s_loc.reshape(B, H, rows, ncb, NA_QC, kh * NA_KC)
    s_ctx = jnp.einsum('brnqhd,bmhd->bhrnqm', qg, k_ctx).astype(jnp.float32) * scale
    p = jax.nn.softmax(jnp.concatenate([s_loc, s_ctx], axis=-1), axis=-1).astype(v.dtype)
    p_loc = p[..., :kh * NA_KC].reshape(B, H, rows, ncb, NA_QC, kh, NA_KC)
    p_ctx = p[..., kh * NA_KC:]
    o = jnp.einsum('bhrnqij,brinjhd->brnqhd', p_loc, vg) + jnp.einsum('bhrnqm,bmhd->brnqhd', p_ctx, v_ctx)
    return o.reshape(B, S, H * d)


def window_attention(q, k, v, k_ctx, v_ctx, sink):
    B, S, Hq, d = q.shape
    G = k.shape[2]
    rep = Hq // G
    L = k_ctx.shape[1]
    nb = S // SWA_BLOCK
    span = 3 * SWA_BLOCK
    scale = d ** -0.5
    qb = q.reshape(B, nb, SWA_BLOCK, G, rep, d)
    pad = ((0, 0), (SWA_BLOCK, SWA_BLOCK), (0, 0), (0, 0))
    kp = jnp.pad(k, pad)
    vp = jnp.pad(v, pad)
    idx = jnp.arange(nb)[:, None] * SWA_BLOCK + jnp.arange(span)[None, :]
    kwin = kp[:, idx]
    vwin = vp[:, idx]
    kpos = idx - SWA_BLOCK
    qpos = jnp.arange(nb)[:, None] * SWA_BLOCK + jnp.arange(SWA_BLOCK)[None, :]
    ok = (kpos[:, None, :] >= 0) & (kpos[:, None, :] < S) & (jnp.abs(qpos[:, :, None] - kpos[:, None, :]) <= SWA_WINDOW)
    s_loc = jnp.einsum('bnqgrd,bnkgd->bgrnqk', qb, kwin).astype(jnp.float32) * scale
    s_loc = jnp.where(ok, s_loc, -jnp.inf)
    s_ctx = jnp.einsum('bnqgrd,bmgd->bgrnqm', qb, k_ctx).astype(jnp.float32) * scale
    s_sink = jnp.broadcast_to(sink.astype(jnp.float32).reshape(G, rep)[None, :, :, None, None, None], s_loc.shape[:-1] + (1,))
    p = jax.nn.softmax(jnp.concatenate([s_loc, s_ctx, s_sink], axis=-1), axis=-1).astype(v.dtype)
    p_loc = p[..., :span]
    p_ctx = p[..., span:span + L]
    o = jnp.einsum('bgrnqk,bnkgd->bnqgrd', p_loc, vwin) + jnp.einsum('bgrnqm,bmgd->bnqgrd', p_ctx, v_ctx)
    return o.reshape(B, S, Hq * d)


def context_attention(q, k, v, sink=None):
    B, L, Hq, d = q.shape
    G = k.shape[2]
    rep = Hq // G
    qg = q.reshape(B, L, G, rep, d)
    s = jnp.einsum('blgrd,bmgd->bgrlm', qg, k).astype(jnp.float32) * (d ** -0.5)
    if sink is not None:
        s_sink = jnp.broadcast_to(sink.astype(jnp.float32).reshape(G, rep)[None, :, :, None, None], s.shape[:-1] + (1,))
        s = jnp.concatenate([s, s_sink], axis=-1)
    p = jax.nn.softmax(s, axis=-1)[..., :L].astype(v.dtype)
    o = jnp.einsum('bgrlm,bmgd->blgrd', p, v)
    return o.reshape(B, L, Hq * d)


def merge_branches(z, ya, yb, yc, w_out):
    g = jax.nn.sigmoid(z[..., OFF_G:OFF_G + 3 * D_MODEL]).reshape(*z.shape[:-1], 3, D_MODEL)
    m = g[..., 0, :] * ya + g[..., 1, :] * yb + g[..., 2, :] * yc
    return m @ w_out


def peer_ffn(h, w_q, sub_keys, u_tab, v_tab):
    lead = h.shape[:-1]
    hf = h.reshape(-1, D_MODEL)
    T = hf.shape[0]
    q = (hf @ w_q).reshape(T, PEER_HEADS, 2, PEER_DKEY // 2)
    s = jnp.einsum('thpc,hpkc->thpk', q, sub_keys).astype(jnp.float32)
    top_s, top_i = lax.top_k(s, PEER_TOPK)
    cand_s = top_s[:, :, 0, :, None] + top_s[:, :, 1, None, :]
    cand_i = top_i[:, :, 0, :, None] * PEER_NKEYS + top_i[:, :, 1, None, :]
    best_s, best_j = lax.top_k(cand_s.reshape(T, PEER_HEADS, PEER_TOPK * PEER_TOPK), PEER_TOPK)
    expert = jnp.take_along_axis(cand_i.reshape(T, PEER_HEADS, PEER_TOPK * PEER_TOPK), best_j, axis=-1)
    gate = jax.nn.softmax(best_s, axis=-1).astype(h.dtype)
    hk = PEER_HEADS * PEER_TOPK
    nc = T // PEER_CHUNK

    def chunk(args):
        xc, ec, gc = args
        u = u_tab[ec]
        vv = v_tab[ec]
        a = jnp.einsum('ced,cd->ce', u, xc)
        return jnp.einsum('ce,ced->cd', gc * jax.nn.gelu(a, approximate=False), vv)

    out = lax.map(chunk, (hf.reshape(nc, PEER_CHUNK, D_MODEL),
                          expert.reshape(nc, PEER_CHUNK, hk),
                          gate.reshape(nc, PEER_CHUNK, hk)))
    return out.reshape(*lead, D_MODEL)


def setup_inputs(seed: int = 0) -> dict:
    key = jax.random.key(seed)
    ks = jax.random.split(key, 24)
    f32 = jnp.float32
    nrm = lambda k, shape, s: jax.random.normal(k, shape, f32) * s
    return {
        'x': nrm(ks[0], (BATCH, SEQ, D_MODEL), 1.0),
        'c': nrm(ks[1], (BATCH, D_MODEL), 1.0),
        'ctx': nrm(ks[2], (BATCH, CTX_LEN, D_MODEL), 1.0),
        'c_ctx': nrm(ks[3], (D_MODEL,), 1.0),
        'ada_w': nrm(ks[4], (DEPTH, D_MODEL, 6 * D_MODEL), D_MODEL ** -0.5),
        'ada_b': nrm(ks[5], (DEPTH, 6 * D_MODEL), 0.02),
        'norm1_g': 1.0 + nrm(ks[6], (DEPTH, D_MODEL), 0.02),
        'norm2_g': 1.0 + nrm(ks[7], (DEPTH, D_MODEL), 0.02),
        'w_in': nrm(ks[8], (DEPTH, D_MODEL, IN_COLS), D_MODEL ** -0.5),
        'pool_w': nrm(ks[9], (DEPTH, POOL_GROUPS, POOL_GROUP_W, POOL_GROUP_W), POOL_GROUP_W ** -0.5),
        'pool_scale': 1.0 + nrm(ks[10], (DEPTH, POOL_WIDTH), 0.1),
        'na_rpb': nrm(ks[11], (DEPTH, NA_HEADS, 2 * NA_KH - 1, 2 * NA_KW - 1), 0.1),
        'swa_sink': nrm(ks[12], (DEPTH, SWA_HEADS), 0.5),
        'w_branch_a': nrm(ks[13], (DEPTH, POOL_WIDTH, D_MODEL), POOL_WIDTH ** -0.5),
        'w_branch_b': nrm(ks[14], (DEPTH, NA_WIDTH, D_MODEL), NA_WIDTH ** -0.5),
        'w_branch_c': nrm(ks[15], (DEPTH, SWA_Q_WIDTH, D_MODEL), SWA_Q_WIDTH ** -0.5),
        'w_out': nrm(ks[16], (DEPTH, D_MODEL, D_MODEL), D_MODEL ** -0.5),
        'peer_wq': nrm(ks[17], (DEPTH, D_MODEL, PEER_HEADS * PEER_DKEY), D_MODEL ** -0.5),
        'peer_keys': nrm(ks[18], (DEPTH, PEER_HEADS, 2, PEER_NKEYS, PEER_DKEY // 2), (PEER_DKEY // 2) ** -0.5),
        'peer_u': nrm(ks[19], (DEPTH, PEER_EXPERTS, D_MODEL), D_MODEL ** -0.5),
        'peer_v': nrm(ks[20], (DEPTH, PEER_EXPERTS, D_MODEL), 0.5),
        'final_g': 1.0 + nrm(ks[21], (D_MODEL,), 0.02),
    }


def reference(x, c, ctx, c_ctx, ada_w, ada_b, norm1_g, norm2_g, w_in, pool_w, pool_scale,
              na_rpb, swa_sink, w_branch_a, w_branch_b, w_branch_c, w_out,
              peer_wq, peer_keys, peer_u, peer_v, final_g):
    S = x.shape[1]
    t = jnp.arange(S)
    row = t // GRID_W
    col = t % GRID_W
    xc = ctx
    for l in range(DEPTH):
        last = l == DEPTH - 1
        mod_l = jax.nn.silu(c) @ ada_w[l] + ada_b[l]
        mod_c = jax.nn.silu(c_ctx) @ ada_w[l] + ada_b[l]
        sh1, sc1, gt1, sh2, sc2, gt2 = jnp.split(mod_l[:, None, :], 6, axis=-1)
        csh1, csc1, cgt1, csh2, csc2, cgt2 = jnp.split(mod_c, 6, axis=-1)

        h = modulate(rms_norm(x, norm1_g[l]), sh1, sc1)
        hc = modulate(rms_norm(xc, norm1_g[l]), csh1, csc1)
        z = h @ w_in[l]
        zc = hc @ (w_in[l][:, :CTX_KV_COLS] if last else w_in[l])
        kn_c = heads(zc, OFF_KN, NA_HEADS)
        vn_c = heads(zc, OFF_VN, NA_HEADS)
        kw_c = heads(zc, OFF_KW, SWA_KV_HEADS)
        vw_c = heads(zc, OFF_VW, SWA_KV_HEADS)

        qn = heads(z, OFF_QN, NA_HEADS)
        kn = heads(z, OFF_KN, NA_HEADS)
        vn = heads(z, OFF_VN, NA_HEADS)
        qw = rope_2d(heads(z, OFF_QW, SWA_HEADS), row, col)
        kw = rope_2d(heads(z, OFF_KW, SWA_KV_HEADS), row, col)
        vw = heads(z, OFF_VW, SWA_KV_HEADS)
        ya = pool_mixer(z[..., OFF_A:OFF_G], pool_w[l], pool_scale[l]) @ w_branch_a[l]
        yb = neighbourhood_attention(qn, kn, vn, kn_c, vn_c, na_rpb[l]) @ w_branch_b[l]
        yc = window_attention(qw, kw, vw, kw_c, vw_c, swa_sink[l]) @ w_branch_c[l]
        x = x + gt1 * merge_branches(z, ya, yb, yc, w_out[l])
        if not last:
            ya_c = pool_mixer(zc[..., OFF_A:OFF_G], pool_w[l], pool_scale[l]) @ w_branch_a[l]
            yb_c = context_attention(heads(zc, OFF_QN, NA_HEADS), kn_c, vn_c) @ w_branch_b[l]
            yc_c = context_attention(heads(zc, OFF_QW, SWA_HEADS), kw_c, vw_c, swa_sink[l]) @ w_branch_c[l]
            xc = xc + cgt1 * merge_branches(zc, ya_c, yb_c, yc_c, w_out[l])

        h2 = modulate(rms_norm(x, norm2_g[l]), sh2, sc2)
        x = x + gt2 * peer_ffn(h2, peer_wq[l], peer_keys[l], peer_u[l], peer_v[l])
        if not last:
            h2c = modulate(rms_norm(xc, norm2_g[l]), csh2, csc2)
            xc = xc + cgt2 * peer_ffn(h2c, peer_wq[l], peer_keys[l], peer_u[l], peer_v[l])
    return rms_norm(x, final_g)
```

```python
import functools

import jax
import jax.numpy as jnp
from jax import lax
from jax.experimental import pallas as pl
from jax.experimental.pallas import tpu as pltpu

f32 = jnp.float32
bf16 = jnp.bfloat16

D_MODEL = 1024
GRID_W = 64
HEAD_DIM = 64
NORM_EPS = 1e-6
ROPE_BASE = 10000.0

POOL_WIDTH = 512
POOL_WINDOWS = (2, 4, 8, 16)
POOL_GROUP_W = 128
POOL_HALO = 8

NA_HEADS = 8
NA_KH = 8
NA_KW = 16

SWA_HEADS = 8
SWA_KV_HEADS = 2
SWA_WINDOW = 128
SWA_BLOCK = 128

PEER_HEADS = 8
PEER_NKEYS = 128
PEER_TOPK = 16
PEER_HK = PEER_HEADS * PEER_TOPK

NA_WIDTH = NA_HEADS * HEAD_DIM
SWA_Q_WIDTH = SWA_HEADS * HEAD_DIM
SWA_KV_WIDTH = SWA_KV_HEADS * HEAD_DIM
OFF_KN = 0
OFF_VN = OFF_KN + NA_WIDTH
OFF_KW = OFF_VN + NA_WIDTH
OFF_VW = OFF_KW + SWA_KV_WIDTH
OFF_QN = OFF_VW + SWA_KV_WIDTH
OFF_QW = OFF_QN + NA_WIDTH
OFF_A = OFF_QW + SWA_Q_WIDTH
OFF_G = OFF_A + POOL_WIDTH

LANES = 128
SUBLANES = 8
NEG_BIG = -1e30
MIB = 1024 * 1024

SEG_KN = ("knvn", 0, 1024)
SEG_SW = ("sw", 1024, 512)
SEG_Q = ("q", 1536, 1024)
SEG_A = ("a", 2560, 512)
SEG_G = ("g", 3072, 3072)
SEGS_ALL = (SEG_KN, SEG_SW, SEG_Q, SEG_A, SEG_G)
SEGS_KV = (SEG_KN, SEG_SW)
SEG_DTYPE = {"knvn": bf16, "sw": bf16, "q": bf16, "a": f32, "g": bf16}
COL_CHUNK = 512


def _params(sem, vmem_mib):
    return pltpu.CompilerParams(dimension_semantics=sem, vmem_limit_bytes=vmem_mib * MIB)


def _const_spec(shape):
    nd = len(shape)
    return pl.BlockSpec(shape, lambda *_: (0,) * nd, pipeline_mode=pl.Buffered(1))


def _dot(a, b):
    return jnp.dot(a, b, preferred_element_type=f32)


def _dot_nt(a, b):
    return lax.dot_general(a, b, (((1,), (1,)), ((), ())), preferred_element_type=f32)


def _split(a):
    hi = a.astype(bf16)
    lo = (a - hi.astype(f32)).astype(bf16)
    return hi, lo


def _sigmoid(z):
    return 1.0 / (1.0 + jnp.exp(-z))


def _rms_mod(x, g, sh, sc):
    ms = jnp.mean(x * x, axis=-1, keepdims=True)
    h = x * lax.rsqrt(ms + NORM_EPS) * g
    return h * (1.0 + sc) + sh


def _mod_kernel(c_ref, w_ref, b_ref, o_ref):
    c = c_ref[...]
    s = c * _sigmoid(c)
    sh, sl = _split(s)
    wh, wl = _split(w_ref[0])
    o_ref[0] = _dot(sh, wh) + _dot(sh, wl) + _dot(sl, wh) + b_ref[0]


def _modulation(cpad, ada_w, ada_b):
    depth, d, n = ada_w.shape
    tn = 1536
    return pl.pallas_call(
        _mod_kernel,
        grid=(depth, n // tn),
        in_specs=[
            pl.BlockSpec((SUBLANES, d), lambda l, j: (0, 0)),
            pl.BlockSpec((1, d, tn), lambda l, j: (l, 0, j)),
            pl.BlockSpec((1, 1, tn), lambda l, j: (l, 0, j)),
        ],
        out_specs=pl.BlockSpec((1, SUBLANES, tn), lambda l, j: (l, 0, j)),
        out_shape=jax.ShapeDtypeStruct((depth, SUBLANES, n), f32),
        compiler_params=_params(("arbitrary", "arbitrary"), 40),
        name="adaln_mod",
    )(cpad, ada_w, ada_b.reshape(depth, 1, n))


def _rope(z, cos, s1, s2):
    return z * cos + pltpu.roll(z, LANES - 16, 1) * s1 + pltpu.roll(z, 16, 1) * s2


def _inproj_kernel(segs, rope, x_ref, g_ref, sh_ref, sc_ref, w_ref, *rest):
    if rope:
        cos_ref, s1_ref, s2_ref = rest[:3]
        outs = rest[3:]
        cos, s1, s2 = cos_ref[...], s1_ref[...], s2_ref[...]
    else:
        outs = rest
    hb = _rms_mod(x_ref[0], g_ref[...], sh_ref[0], sc_ref[0]).astype(bf16)
    for (name, c0, width), o_ref in zip(segs, outs):
        for j in range(0, width, COL_CHUNK):
            z = _dot(hb, w_ref[:, c0 + j:c0 + j + COL_CHUNK])
            if name == "g":
                z = _sigmoid(z)
            if rope and name in ("sw", "q"):
                n_rot = 2 if name == "sw" else (4 if j == COL_CHUNK else 0)
                slabs = [z[:, s * LANES:(s + 1) * LANES] for s in range(COL_CHUNK // LANES)]
                slabs = [_rope(sl, cos, s1, s2) if s < n_rot else sl for s, sl in enumerate(slabs)]
                z = jnp.concatenate(slabs, axis=1)
            o_ref[0, :, j:j + COL_CHUNK] = z.astype(o_ref.dtype)


def _inproj(x, g, sh, sc, w_aug, segs, rope_tabs, tm):
    b, s, d = x.shape
    rope = rope_tabs is not None
    in_specs = [
        pl.BlockSpec((1, tm, d), lambda i, j: (i, j, 0)),
        _const_spec((1, d)),
        pl.BlockSpec((1, 1, d), lambda i, j: (i, 0, 0)),
        pl.BlockSpec((1, 1, d), lambda i, j: (i, 0, 0)),
        _const_spec(w_aug.shape),
    ]
    args = [x, g.reshape(1, d), sh, sc, w_aug]
    if rope:
        in_specs += [pl.BlockSpec((tm, LANES), lambda i, j: (j, 0))] * 3
        args += list(rope_tabs)
    out_specs = [pl.BlockSpec((1, tm, w), lambda i, j: (i, j, 0)) for _, _, w in segs]
    out_shape = [jax.ShapeDtypeStruct((b, s, w), SEG_DTYPE[n]) for n, _, w in segs]
    return pl.pallas_call(
        functools.partial(_inproj_kernel, segs, rope),
        grid=(b, s // tm),
        in_specs=in_specs,
        out_specs=out_specs,
        out_shape=out_shape,
        compiler_params=_params(("parallel", "parallel"), 52),
        name="inproj",
    )(*args)


def _pool_kernel(seq, tt, a_ref, pw_ref, ps_ref, o_ref):
    t0 = pl.multiple_of(pl.program_id(1) * tt, tt)
    cur = a_ref[0, pl.ds(t0, tt), :]
    p0 = pl.multiple_of(jnp.maximum(t0 - POOL_HALO, 0), POOL_HALO)
    n0 = pl.multiple_of(jnp.minimum(t0 + tt, seq - POOL_HALO), POOL_HALO)
    prev = jnp.where(t0 > 0, a_ref[0, pl.ds(p0, POOL_HALO), :], 0.0)
    nxt = jnp.where(t0 + tt < seq, a_ref[0, pl.ds(n0, POOL_HALO), :], 0.0)
    ext = jnp.concatenate([prev, cur, nxt], axis=0)
    n_ext = tt + 2 * POOL_HALO
    tpos = t0 + lax.broadcasted_iota(jnp.int32, (tt, 1), 0)
    for g, w in enumerate(POOL_WINDOWS):
        sl = slice(g * POOL_GROUP_W, (g + 1) * POOL_GROUP_W)
        eg = ext[:, sl]
        acc = jnp.zeros((tt, POOL_GROUP_W), f32)
        for k in range(-(w // 2), w - w // 2):
            shifted = eg if k == 0 else pltpu.roll(eg, (-k) % n_ext, 0)
            acc = acc + shifted[POOL_HALO:POOL_HALO + tt]
        lo = jnp.clip(tpos - w // 2, 0, seq)
        hi = jnp.clip(tpos - w // 2 + w, 0, seq)
        pooled = acc / (hi - lo).astype(f32) - cur[:, sl]
        y = _dot(pooled.astype(bf16), pw_ref[g]) * ps_ref[:, sl]
        o_ref[0, :, sl] = y.astype(o_ref.dtype)


def _pool(a, pool_w, pool_scale, tt):
    b, s, w = a.shape
    return pl.pallas_call(
        functools.partial(_pool_kernel, s, tt),
        grid=(b, s // tt),
        in_specs=[
            pl.BlockSpec((1, s, w), lambda i, j: (i, 0, 0)),
            _const_spec(pool_w.shape),
            _const_spec((1, w)),
        ],
        out_specs=pl.BlockSpec((1, tt, w), lambda i, j: (i, j, 0)),
        out_shape=jax.ShapeDtypeStruct((b, s, w), bf16),
        compiler_params=_params(("parallel", "arbitrary"), 40),
        name="pool_mixer",
    )(a, pool_w, pool_scale.reshape(1, w))


def _half_masks():
    lane = lax.broadcasted_iota(jnp.int32, (1, LANES), 1)
    lo = lane < HEAD_DIM
    return lo, jnp.logical_not(lo)


def _attend(qm, keys, vals, biases, extra_logit):
    scale = HEAD_DIM ** -0.5
    logits = []
    for k, bias in zip(keys, biases):
        s = _dot_nt(qm, k) * scale
        if bias is not None:
            s = s + bias
        logits.append(s)
    m = functools.reduce(jnp.maximum, [jnp.max(s, axis=-1, keepdims=True) for s in logits])
    if extra_logit is not None:
        m = jnp.maximum(m, extra_logit)
    den = None
    out = None
    for s, v in zip(logits, vals):
        e = jnp.exp(s - m)
        d = jnp.sum(e, axis=-1, keepdims=True)
        o = _dot(e.astype(bf16), v)
        den = d if den is None else den + d
        out = o if out is None else out + o
    if extra_logit is not None:
        den = den + jnp.exp(extra_logit - m)
    return out / den


def _na_kernel(rows, q_ref, k_ref, v_ref, kc_ref, vc_ref, bias_ref, o_ref):
    r = pl.program_id(1)
    k0 = pl.multiple_of(jnp.clip(r - NA_KH // 2, 0, rows - NA_KH) * GRID_W, GRID_W)
    nk = NA_KH * GRID_W
    lo, hi = _half_masks()
    for p in range(NA_HEADS // 2):
        sl = slice(p * LANES, (p + 1) * LANES)
        qp = q_ref[0, :, sl]
        kp = k_ref[0, pl.ds(k0, nk), sl]
        vp = v_ref[0, pl.ds(k0, nk), sl]
        kcp = kc_ref[0, :, sl]
        vcp = vc_ref[0, :, sl]
        halves = []
        for half, msk in enumerate((lo, hi)):
            qm = jnp.where(msk, qp, jnp.zeros_like(qp))
            halves.append(_attend(qm, (kp, kcp), (vp, vcp), (bias_ref[2 * p + half, 0], None), None))
        o_ref[0, :, sl] = jnp.where(lo, halves[0], halves[1]).astype(o_ref.dtype)


def _na(q, knvn, knvn_c, bias_tbl):
    b, s, _ = q.shape
    rows = s // GRID_W
    lc = knvn_c.shape[1]
    nb = NA_WIDTH // NA_WIDTH

    def cls(r):
        return r - jnp.clip(r - NA_KH // 2, 0, rows - NA_KH)

    return pl.pallas_call(
        functools.partial(_na_kernel, rows),
        grid=(b, rows),
        in_specs=[
            pl.BlockSpec((1, GRID_W, NA_WIDTH), lambda i, r: (i, r, 0)),
            pl.BlockSpec((1, s, NA_WIDTH), lambda i, r: (i, 0, 0)),
            pl.BlockSpec((1, s, NA_WIDTH), lambda i, r: (i, 0, nb)),
            pl.BlockSpec((1, lc, NA_WIDTH), lambda i, r: (i, 0, 0)),
            pl.BlockSpec((1, lc, NA_WIDTH), lambda i, r: (i, 0, nb)),
            pl.BlockSpec((NA_HEADS, 1, GRID_W, NA_KH * GRID_W), lambda i, r: (0, cls(r), 0, 0)),
        ],
        out_specs=pl.BlockSpec((1, GRID_W, NA_WIDTH), lambda i, r: (i, r, 0)),
        out_shape=jax.ShapeDtypeStruct((b, s, NA_WIDTH), bf16),
        compiler_params=_params(("parallel", "arbitrary"), 48),
        name="nbr_attn",
    )(q, knvn, knvn, knvn_c, knvn_c, bias_tbl)


def _swa_kernel(seq, q_ref, sw_ref, swc_ref, sink_ref, o_ref):
    n = pl.program_id(1)
    span = 3 * SWA_BLOCK
    start = pl.multiple_of(jnp.clip((n - 1) * SWA_BLOCK, 0, seq - span), SWA_BLOCK)
    qpos = n * SWA_BLOCK + lax.broadcasted_iota(jnp.int32, (SWA_BLOCK, 1), 0)
    kpos = start + lax.broadcasted_iota(jnp.int32, (1, span), 1)
    bias = jnp.where(jnp.abs(qpos - kpos) <= SWA_WINDOW, 0.0, NEG_BIG)
    lo, hi = _half_masks()
    rep = SWA_HEADS // SWA_KV_HEADS
    outs = []
    for h in range(SWA_HEADS):
        g = h // rep
        kd = sw_ref[0, pl.ds(start, span), g * LANES:(g + 1) * LANES]
        vd = sw_ref[0, pl.ds(start, span), (2 + g) * LANES:(3 + g) * LANES]
        kcd = swc_ref[0, :, g * LANES:(g + 1) * LANES]
        vcd = swc_ref[0, :, (2 + g) * LANES:(3 + g) * LANES]
        qp = q_ref[0, :, (h // 2) * LANES:(h // 2 + 1) * LANES]
        qm = jnp.where(hi if h % 2 else lo, qp, jnp.zeros_like(qp))
        outs.append(_attend(qm, (kd, kcd), (vd, vcd), (bias, None), sink_ref[h]))
    for p in range(SWA_HEADS // 2):
        o_ref[0, :, p * LANES:(p + 1) * LANES] = jnp.where(lo, outs[2 * p], outs[2 * p + 1]).astype(o_ref.dtype)


def _swa(q, sw, sw_c, sink):
    b, s, _ = q.shape
    lc = sw_c.shape[1]
    return pl.pallas_call(
        functools.partial(_swa_kernel, s),
        grid=(b, s // SWA_BLOCK),
        in_specs=[
            pl.BlockSpec((1, SWA_BLOCK, SWA_Q_WIDTH), lambda i, n: (i, n, 1)),
            pl.BlockSpec((1, s, 512), lambda i, n: (i, 0, 0)),
            pl.BlockSpec((1, lc, 512), lambda i, n: (i, 0, 0)),
            pl.BlockSpec(memory_space=pltpu.SMEM),
        ],
        out_specs=pl.BlockSpec((1, SWA_BLOCK, SWA_Q_WIDTH), lambda i, n: (i, n, 0)),
        out_shape=jax.ShapeDtypeStruct((b, s, SWA_Q_WIDTH), bf16),
        compiler_params=_params(("parallel", "arbitrary"), 40),
        name="win_attn",
    )(q, sw, sw_c, sink)


def _ctx_attn_kernel(q_ref, kv_ref, sw_ref, sink_ref, ob_ref, oc_ref):
    lo, hi = _half_masks()
    rep = SWA_HEADS // SWA_KV_HEADS
    for p in range(NA_HEADS // 2):
        sl = slice(p * LANES, (p + 1) * LANES)
        qp = q_ref[0, :, sl]
        kp = kv_ref[0, :, sl]
        vp = kv_ref[0, :, NA_WIDTH + p * LANES:NA_WIDTH + (p + 1) * LANES]
        halves = [_attend(jnp.where(msk, qp, jnp.zeros_like(qp)), (kp,), (vp,), (None,), None) for msk in (lo, hi)]
        ob_ref[0, :, sl] = jnp.where(lo, halves[0], halves[1]).astype(ob_ref.dtype)
    outs = []
    for h in range(SWA_HEADS):
        g = h // rep
        kd = sw_ref[0, :, g * LANES:(g + 1) * LANES]
        vd = sw_ref[0, :, (2 + g) * LANES:(3 + g) * LANES]
        qp = q_ref[0, :, NA_WIDTH + (h // 2) * LANES:NA_WIDTH + (h // 2 + 1) * LANES]
        qm = jnp.where(hi if h % 2 else lo, qp, jnp.zeros_like(qp))
        outs.append(_attend(qm, (kd,), (vd,), (None,), sink_ref[h]))
    for p in range(SWA_HEADS // 2):
        oc_ref[0, :, p * LANES:(p + 1) * LANES] = jnp.where(lo, outs[2 * p], outs[2 * p + 1]).astype(oc_ref.dtype)


def _ctx_attn(q_c, knvn_c, sw_c, sink):
    b, lc, _ = q_c.shape
    spec = lambda w: pl.BlockSpec((1, lc, w), lambda i: (i, 0, 0))
    return pl.pallas_call(
        _ctx_attn_kernel,
        grid=(b,),
        in_specs=[spec(1024), spec(1024), spec(512), pl.BlockSpec(memory_space=pltpu.SMEM)],
        out_specs=[spec(NA_WIDTH), spec(SWA_Q_WIDTH)],
        out_shape=[jax.ShapeDtypeStruct((b, lc, NA_WIDTH), bf16), jax.ShapeDtypeStruct((b, lc, SWA_Q_WIDTH), bf16)],
        compiler_params=_params(("parallel",), 40),
        name="ctx_attn",
    )(q_c, knvn_c, sw_c, sink)


def _merge_kernel(x_ref, ya_ref, ob_ref, oc_ref, g_ref, gt_ref, wa_ref, wb_ref, wc_ref, wo_ref, o_ref):
    d = D_MODEL
    m = g_ref[0, :, 0:d].astype(f32) * _dot(ya_ref[0], wa_ref[...])
    m = m + g_ref[0, :, d:2 * d].astype(f32) * _dot(ob_ref[0], wb_ref[...])
    m = m + g_ref[0, :, 2 * d:3 * d].astype(f32) * _dot(oc_ref[0], wc_ref[...])
    o_ref[0] = x_ref[0] + gt_ref[0] * _dot(m.astype(bf16), wo_ref[...])


def _merge(x, ya, ob, oc, g, gt, wa, wb, wc, wo, tm):
    b, s, d = x.shape
    tok = lambda w: pl.BlockSpec((1, tm, w), lambda i, j: (i, j, 0))
    return pl.pallas_call(
        _merge_kernel,
        grid=(b, s // tm),
        in_specs=[tok(d), tok(512), tok(512), tok(512), tok(3 * d),
                  pl.BlockSpec((1, 1, d), lambda i, j: (i, 0, 0)),
                  _const_spec(wa.shape), _const_spec(wb.shape), _const_spec(wc.shape), _const_spec(wo.shape)],
        out_specs=tok(d),
        out_shape=jax.ShapeDtypeStruct((b, s, d), f32),
        compiler_params=_params(("parallel", "parallel"), 48),
        name="merge_out",
    )(x, ya, ob, oc, g, gt, wa, wb, wc, wo)


def _topk_rows(s, k):
    n = s.shape[0]
    rid = lax.broadcasted_iota(jnp.int32, s.shape, 0)
    vals, idxs = [], []
    for _ in range(k):
        m = jnp.max(s, axis=0, keepdims=True)
        i = jnp.min(jnp.where(s == m, rid, n), axis=0, keepdims=True)
        vals.append(m)
        idxs.append(i)
        s = jnp.where(rid == i, -jnp.inf, s)
    return jnp.concatenate(vals, axis=0), jnp.concatenate(idxs, axis=0)


def _pick_rows(tab, sel):
    rid = lax.broadcasted_iota(jnp.int32, tab.shape, 0)
    rows = []
    for k in range(sel.shape[0]):
        rows.append(jnp.sum(jnp.where(rid == sel[k:k + 1, :], tab, 0.0), axis=0, keepdims=True))
    return jnp.concatenate(rows, axis=0)


def _route_kernel(x_ref, g_ref, sh_ref, sc_ref, wqh_ref, wql_ref, kh_ref, kl_ref, h2_ref, i_ref, j_ref, gate_ref):
    h2 = _rms_mod(x_ref[0], g_ref[...], sh_ref[0], sc_ref[0])
    h2_ref[0] = h2.astype(bf16)
    hh, hl = _split(h2)
    k = PEER_TOPK
    i_rows, j_rows, g_rows = [], [], []
    for h in range(PEER_HEADS):
        top_s, top_i = [], []
        for p in range(2):
            hp = 2 * h + p
            cs = slice(hp * PEER_NKEYS, (hp + 1) * PEER_NKEYS)
            q = _dot(hh, wqh_ref[:, cs]) + _dot(hh, wql_ref[:, cs]) + _dot(hl, wqh_ref[:, cs])
            qh, ql = _split(q)
            st = _dot_nt(kh_ref[hp], qh) + _dot_nt(kh_ref[hp], ql) + _dot_nt(kl_ref[hp], qh)
            ts, ti = _topk_rows(st, k)
            top_s.append(ts)
            top_i.append(ti.astype(f32))
        cand = jnp.concatenate([top_s[0][a:a + 1, :] + top_s[1] for a in range(k)], axis=0)
        best_s, best_j = _topk_rows(cand, k)
        i_rows.append(_pick_rows(top_i[0], lax.shift_right_logical(best_j, 4)))
        j_rows.append(_pick_rows(top_i[1], lax.bitwise_and(best_j, k - 1)))
        e = jnp.exp(best_s - best_s[0:1, :])
        g_rows.append(e / jnp.sum(e, axis=0, keepdims=True))
    i_ref[0] = jnp.concatenate(i_rows, axis=0).T
    j_ref[0] = jnp.concatenate(j_rows, axis=0).T
    gate_ref[0] = jnp.concatenate(g_rows, axis=0).T


def _route(x, g, sh, sc, wqh, wql, kh, kl, tt):
    b, s, d = x.shape
    tok = lambda w: pl.BlockSpec((1, tt, w), lambda i, j: (i, j, 0))
    return pl.pallas_call(
        _route_kernel,
        grid=(b, s // tt),
        in_specs=[tok(d), _const_spec((1, d)),
                  pl.BlockSpec((1, 1, d), lambda i, j: (i, 0, 0)),
                  pl.BlockSpec((1, 1, d), lambda i, j: (i, 0, 0)),
                  _const_spec(wqh.shape), _const_spec(wql.shape), _const_spec(kh.shape), _const_spec(kl.shape)],
        out_specs=[tok(d), tok(PEER_HK), tok(PEER_HK), tok(PEER_HK)],
        out_shape=[jax.ShapeDtypeStruct((b, s, d), bf16)] + [jax.ShapeDtypeStruct((b, s, PEER_HK), f32)] * 3,
        compiler_params=_params(("parallel", "parallel"), 48),
        name="peer_route",
    )(x, g.reshape(1, d), sh, sc, wqh, wql, kh, kl)


EXPERT_BLOCK = 1024
I_PER_BLOCK = EXPERT_BLOCK // PEER_NKEYS
N_EXPERT_BLOCKS = PEER_NKEYS * PEER_NKEYS // EXPERT_BLOCK
BUILD_UNROLL = 8


def _gelu(a):
    return 0.5 * a * (1.0 + lax.erf(a * (2.0 ** -0.5)))


def _expert_kernel(tt, final, x_ref, h2_ref, i_ref, j_ref, gate_ref, gt_ref, ut_ref, v_ref, fg_ref, o_ref, g_buf, acc_ref):
    ib = pl.program_id(2)

    @pl.when(ib == 0)
    def _build():
        sub = lax.broadcasted_iota(jnp.int32, (PEER_NKEYS, PEER_HK), 0).astype(f32)

        def body(step, carry):
            for u in range(BUILD_UNROLL):
                t = step * BUILD_UNROLL + u
                irow = i_ref[0, pl.ds(t, 1), :]
                jrow = j_ref[0, pl.ds(t, 1), :]
                grow = gate_ref[0, pl.ds(t, 1), :]
                pt = (sub == irow).astype(bf16)
                qt = jnp.where(sub == jrow, grow, 0.0).astype(bf16)
                gt = _dot_nt(pt, qt)
                g_buf[:, pl.ds(pl.multiple_of(t * I_PER_BLOCK, I_PER_BLOCK), I_PER_BLOCK), :] = gt.reshape(
                    N_EXPERT_BLOCKS, I_PER_BLOCK, PEER_NKEYS)
            return carry

        lax.fori_loop(0, tt // BUILD_UNROLL, body, 0)
        acc_ref[...] = jnp.zeros_like(acc_ref)

    a = _dot(h2_ref[0], ut_ref[...])
    pieces = []
    for k in range(I_PER_BLOCK):
        gk = g_buf[ib, pl.ds(k, tt, stride=I_PER_BLOCK), :]
        pieces.append((gk * _gelu(a[:, k * PEER_NKEYS:(k + 1) * PEER_NKEYS])).astype(bf16))
    acc_ref[...] += _dot(jnp.concatenate(pieces, axis=1), v_ref[...])

    @pl.when(ib == N_EXPERT_BLOCKS - 1)
    def _finish():
        y = x_ref[0] + gt_ref[0] * acc_ref[...]
        if final:
            ms = jnp.mean(y * y, axis=-1, keepdims=True)
            y = y * lax.rsqrt(ms + NORM_EPS) * fg_ref[...]
        o_ref[0] = y


def _experts(x, h2, idx_i, idx_j, gate, gt, ut, v, final_g, final, tt):
    b, s, d = x.shape
    tok = lambda w: pl.BlockSpec((1, tt, w), lambda i, j, e: (i, j, 0))
    return pl.pallas_call(
        functools.partial(_expert_kernel, tt, final),
        grid=(b, s // tt, N_EXPERT_BLOCKS),
        in_specs=[tok(d), tok(d), tok(PEER_HK), tok(PEER_HK), tok(PEER_HK),
                  pl.BlockSpec((1, 1, d), lambda i, j, e: (i, 0, 0)),
                  pl.BlockSpec((d, EXPERT_BLOCK), lambda i, j, e: (0, e)),
                  pl.BlockSpec((EXPERT_BLOCK, d), lambda i, j, e: (e, 0)),
                  pl.BlockSpec((1, d), lambda i, j, e: (0, 0))],
        out_specs=tok(d),
        out_shape=jax.ShapeDtypeStruct((b, s, d), f32),
        scratch_shapes=[pltpu.VMEM((N_EXPERT_BLOCKS, tt * I_PER_BLOCK, PEER_NKEYS), f32),
                        pltpu.VMEM((tt, d), f32)],
        compiler_params=_params(("parallel", "parallel", "arbitrary"), 52),
        name="peer_experts",
    )(x, h2, idx_i, idx_j, gate, gt, ut, v, final_g.reshape(1, d))


def _rope_tables(seq):
    t = jnp.arange(seq)
    pos = jnp.stack([t // GRID_W, t % GRID_W], axis=1).astype(f32)
    f = HEAD_DIM // 4
    inv = ROPE_BASE ** (-jnp.arange(f, dtype=f32) / f)
    lane = jnp.arange(LANES)
    l64 = lane % HEAD_DIM
    ang = pos[:, (l64 // (2 * f))] * inv[l64 % f][None, :]
    first = (l64 % (2 * f)) < f
    cos = jnp.cos(ang)
    sin = jnp.sin(ang)
    return cos, jnp.where(first[None, :], -sin, 0.0), jnp.where(first[None, :], 0.0, sin)


def _na_bias_table(rpb):
    cls = jnp.arange(NA_KH)[:, None]
    kr = jnp.arange(NA_KH)[None, :]
    dr = kr - cls + NA_KH - 1
    qc = jnp.arange(GRID_W)[:, None]
    kc = jnp.arange(GRID_W)[None, :]
    cstart = jnp.clip(qc - NA_KW // 2, 0, GRID_W - NA_KW)
    ok = (kc >= cstart) & (kc < cstart + NA_KW)
    dc = jnp.clip(kc - qc + NA_KW - 1, 0, 2 * NA_KW - 2)
    bias = rpb.astype(f32)[:, dr[:, None, :, None], dc[None, :, None, :]]
    bias = jnp.where(ok[None, None, :, None, :], bias, NEG_BIG)
    return bias.reshape(NA_HEADS, NA_KH, GRID_W, NA_KH * GRID_W)


def _layout_w_in(w):
    c = lambda off, width: w[:, off:off + width]
    hd = HEAD_DIM
    k0, k1 = c(OFF_KW, hd), c(OFF_KW + hd, hd)
    v0, v1 = c(OFF_VW, hd), c(OFF_VW + hd, hd)
    cols = [c(OFF_KN, NA_WIDTH), c(OFF_VN, NA_WIDTH), k0, k0, k1, k1, v0, v0, v1, v1,
            c(OFF_QN, NA_WIDTH), c(OFF_QW, SWA_Q_WIDTH), c(OFF_A, POOL_WIDTH), c(OFF_G, 3 * D_MODEL)]
    return jnp.concatenate(cols, axis=1).astype(bf16)


def _bsplit(a):
    hi = a.astype(bf16)
    return hi, (a - hi.astype(f32)).astype(bf16)


LATENT_TILE = 512
PEER_TILE = 256


def kernel(x, c, ctx, c_ctx, ada_w, ada_b, norm1_g, norm2_g, w_in, pool_w, pool_scale, na_rpb, swa_sink,
           w_branch_a, w_branch_b, w_branch_c, w_out, peer_wq, peer_keys, peer_u, peer_v, final_g):
    b, s, d = x.shape
    lc = ctx.shape[1]
    depth = ada_w.shape[0]
    cpad = jnp.concatenate([c, c_ctx[None, :], jnp.zeros((SUBLANES - b - 1, d), f32)], axis=0)
    mod = _modulation(cpad, ada_w, ada_b)
    rope_tabs = _rope_tables(s)
    xc = ctx
    for l in range(depth):
        last = l == depth - 1
        ml = mod[l, :b].reshape(b, 6, 1, d)
        sh1, sc1, gt1, sh2, sc2, gt2 = (ml[:, i] for i in range(6))
        mc = jnp.broadcast_to(mod[l, b].reshape(1, 6, 1, d), (b, 6, 1, d))
        csh1, csc1, cgt1, csh2, csc2, cgt2 = (mc[:, i] for i in range(6))

        w_aug = _layout_w_in(w_in[l])
        wa, wb, wc, wo = (w[l].astype(bf16) for w in (w_branch_a, w_branch_b, w_branch_c, w_out))
        pw = pool_w[l].astype(bf16)
        bias_tbl = _na_bias_table(na_rpb[l])
        wqh, wql = _bsplit(peer_wq[l])
        kh, kl = _bsplit(peer_keys[l].reshape(2 * PEER_HEADS, PEER_NKEYS, -1))
        ut = peer_u[l].T.astype(bf16)
        vb = peer_v[l].astype(bf16)

        knvn, sw, q, a, g = _inproj(x, norm1_g[l], sh1, sc1, w_aug, SEGS_ALL, rope_tabs, LATENT_TILE)
        if last:
            knvn_c, sw_c = _inproj(xc, norm1_g[l], csh1, csc1, w_aug, SEGS_KV, None, lc)
        else:
            knvn_c, sw_c, q_c, a_c, g_c = _inproj(xc, norm1_g[l], csh1, csc1, w_aug, SEGS_ALL, None, lc)
        ya = _pool(a, pw, pool_scale[l], 256)
        ob = _na(q, knvn, knvn_c, bias_tbl)
        oc = _swa(q, sw, sw_c, swa_sink[l])
        x = _merge(x, ya, ob, oc, g, gt1, wa, wb, wc, wo, LATENT_TILE)
        if not last:
            ya_c = _pool(a_c, pw, pool_scale[l], lc)
            ob_c, oc_c = _ctx_attn(q_c, knvn_c, sw_c, swa_sink[l])
            xc = _merge(xc, ya_c, ob_c, oc_c, g_c, cgt1, wa, wb, wc, wo, lc)

        h2, ii, jj, gate = _route(x, norm2_g[l], sh2, sc2, wqh, wql, kh, kl, PEER_TILE)
        x = _experts(x, h2, ii, jj, gate, gt2, ut, vb, final_g, last, PEER_TILE)
        if not last:
            h2c, iic, jjc, gatec = _route(xc, norm2_g[l], csh2, csc2, wqh, wql, kh, kl, lc)
            xc = _experts(xc, h2c, iic, jjc, gatec, cgt2, ut, vb, final_g, False, lc)
    return x
```

```python
import functools

import jax
import jax.numpy as jnp
import numpy as np
from jax import lax
from jax.experimental import pallas as pl
from jax.experimental.pallas import tpu as pltpu

f32 = jnp.float32
bf16 = jnp.bfloat16

D_MODEL = 1024
GRID_W = 64
HEAD_DIM = 64
NORM_EPS = 1e-6
ROPE_BASE = 10000.0

POOL_WIDTH = 512
POOL_WINDOWS = (2, 4, 8, 16)
POOL_GROUP_W = 128
POOL_HALO = 8

NA_HEADS = 8
NA_KH = 8
NA_KW = 16

SWA_HEADS = 8
SWA_KV_HEADS = 2
SWA_WINDOW = 128
SWA_BLOCK = 128

PEER_HEADS = 8
PEER_NKEYS = 128
PEER_TOPK = 16
PEER_HK = PEER_HEADS * PEER_TOPK

NA_WIDTH = NA_HEADS * HEAD_DIM
SWA_Q_WIDTH = SWA_HEADS * HEAD_DIM
SWA_KV_WIDTH = SWA_KV_HEADS * HEAD_DIM
OFF_KN = 0
OFF_VN = OFF_KN + NA_WIDTH
OFF_KW = OFF_VN + NA_WIDTH
OFF_VW = OFF_KW + SWA_KV_WIDTH
OFF_QN = OFF_VW + SWA_KV_WIDTH
OFF_QW = OFF_QN + NA_WIDTH
OFF_A = OFF_QW + SWA_Q_WIDTH
OFF_G = OFF_A + POOL_WIDTH

LANES = 128
SUBLANES = 8
NEG_BIG = -1e30
MIB = 1024 * 1024

SEG_KN = ("knvn", 0, 1024)
SEG_SW = ("sw", 1024, 512)
SEG_Q = ("q", 1536, 1024)
SEG_A = ("a", 2560, 512)
SEG_G = ("g", 3072, 3072)
SEGS_ALL = (SEG_KN, SEG_SW, SEG_Q, SEG_A, SEG_G)
SEGS_KV = (SEG_KN, SEG_SW)
SEG_DTYPE = {"knvn": bf16, "sw": bf16, "q": bf16, "a": f32, "g": bf16}
COL_CHUNK = 512


def _params(sem, vmem_mib):
    return pltpu.CompilerParams(dimension_semantics=sem, vmem_limit_bytes=vmem_mib * MIB)


def _const_spec(shape):
    nd = len(shape)
    return pl.BlockSpec(shape, lambda *_: (0,) * nd, pipeline_mode=pl.Buffered(1))


def _dot(a, b):
    return jnp.dot(a, b, preferred_element_type=f32)


def _dot_nt(a, b):
    return lax.dot_general(a, b, (((1,), (1,)), ((), ())), preferred_element_type=f32)


def _split(a):
    hi = a.astype(bf16)
    lo = (a - hi.astype(f32)).astype(bf16)
    return hi, lo


def _sigmoid(z):
    return 1.0 / (1.0 + jnp.exp(-z))


def _rms_mod(x, g, sh, sc):
    ms = jnp.mean(x * x, axis=-1, keepdims=True)
    h = x * lax.rsqrt(ms + NORM_EPS) * g
    return h * (1.0 + sc) + sh


def _mod_kernel(c_ref, w_ref, b_ref, o_ref):
    c = c_ref[...]
    s = c * _sigmoid(c)
    sh, sl = _split(s)
    wh, wl = _split(w_ref[0])
    o_ref[0] = _dot(sh, wh) + _dot(sh, wl) + _dot(sl, wh) + b_ref[0]


def _modulation(cpad, ada_w, ada_b):
    depth, d, n = ada_w.shape
    tn = 1536
    return pl.pallas_call(
        _mod_kernel,
        grid=(depth, n // tn),
        in_specs=[
            pl.BlockSpec((SUBLANES, d), lambda l, j: (0, 0)),
            pl.BlockSpec((1, d, tn), lambda l, j: (l, 0, j)),
            pl.BlockSpec((1, 1, tn), lambda l, j: (l, 0, j)),
        ],
        out_specs=pl.BlockSpec((1, SUBLANES, tn), lambda l, j: (l, 0, j)),
        out_shape=jax.ShapeDtypeStruct((depth, SUBLANES, n), f32),
        compiler_params=_params(("arbitrary", "arbitrary"), 40),
        name="adaln_mod",
    )(cpad, ada_w, ada_b.reshape(depth, 1, n))


def _rope(z, cos, s1, s2):
    return z * cos + pltpu.roll(z, LANES - 16, 1) * s1 + pltpu.roll(z, 16, 1) * s2


def _inproj_kernel(segs, rope, x_ref, g_ref, sh_ref, sc_ref, w_ref, *rest):
    if rope:
        cos_ref, s1_ref, s2_ref = rest[:3]
        outs = rest[3:]
        cos, s1, s2 = cos_ref[...], s1_ref[...], s2_ref[...]
    else:
        outs = rest
    hb = _rms_mod(x_ref[0], g_ref[...], sh_ref[0], sc_ref[0]).astype(bf16)
    for (name, c0, width), o_ref in zip(segs, outs):
        for j in range(0, width, COL_CHUNK):
            z = _dot(hb, w_ref[:, c0 + j:c0 + j + COL_CHUNK])
            if name == "g":
                z = _sigmoid(z)
            if rope and name in ("sw", "q"):
                n_rot = 2 if name == "sw" else (4 if j == COL_CHUNK else 0)
                slabs = [z[:, s * LANES:(s + 1) * LANES] for s in range(COL_CHUNK // LANES)]
                slabs = [_rope(sl, cos, s1, s2) if s < n_rot else sl for s, sl in enumerate(slabs)]
                z = jnp.concatenate(slabs, axis=1)
            o_ref[0, :, j:j + COL_CHUNK] = z.astype(o_ref.dtype)


def _inproj(x, g, sh, sc, w_aug, segs, rope_tabs, tm):
    b, s, d = x.shape
    rope = rope_tabs is not None
    in_specs = [
        pl.BlockSpec((1, tm, d), lambda i, j: (i, j, 0)),
        _const_spec((1, d)),
        pl.BlockSpec((1, 1, d), lambda i, j: (i, 0, 0)),
        pl.BlockSpec((1, 1, d), lambda i, j: (i, 0, 0)),
        _const_spec(w_aug.shape),
    ]
    args = [x, g.reshape(1, d), sh, sc, w_aug]
    if rope:
        in_specs += [pl.BlockSpec((tm, LANES), lambda i, j: (j, 0))] * 3
        args += list(rope_tabs)
    out_specs = [pl.BlockSpec((1, tm, w), lambda i, j: (i, j, 0)) for _, _, w in segs]
    out_shape = [jax.ShapeDtypeStruct((b, s, w), SEG_DTYPE[n]) for n, _, w in segs]
    return pl.pallas_call(
        functools.partial(_inproj_kernel, segs, rope),
        grid=(b, s // tm),
        in_specs=in_specs,
        out_specs=out_specs,
        out_shape=out_shape,
        compiler_params=_params(("parallel", "parallel"), 52),
        name="inproj",
    )(*args)


def _pool_kernel(seq, tt, a_ref, pw_ref, ps_ref, o_ref):
    t0 = pl.multiple_of(pl.program_id(1) * tt, tt)
    cur = a_ref[0, pl.ds(t0, tt), :]
    p0 = pl.multiple_of(jnp.maximum(t0 - POOL_HALO, 0), POOL_HALO)
    n0 = pl.multiple_of(jnp.minimum(t0 + tt, seq - POOL_HALO), POOL_HALO)
    prev = jnp.where(t0 > 0, a_ref[0, pl.ds(p0, POOL_HALO), :], 0.0)
    nxt = jnp.where(t0 + tt < seq, a_ref[0, pl.ds(n0, POOL_HALO), :], 0.0)
    ext = jnp.concatenate([prev, cur, nxt], axis=0)
    n_ext = tt + 2 * POOL_HALO
    tpos = t0 + lax.broadcasted_iota(jnp.int32, (tt, 1), 0)
    for g, w in enumerate(POOL_WINDOWS):
        sl = slice(g * POOL_GROUP_W, (g + 1) * POOL_GROUP_W)
        eg = ext[:, sl]
        acc = jnp.zeros((tt, POOL_GROUP_W), f32)
        for k in range(-(w // 2), w - w // 2):
            shifted = eg if k == 0 else pltpu.roll(eg, (-k) % n_ext, 0)
            acc = acc + shifted[POOL_HALO:POOL_HALO + tt]
        lo = jnp.clip(tpos - w // 2, 0, seq)
        hi = jnp.clip(tpos - w // 2 + w, 0, seq)
        pooled = acc / (hi - lo).astype(f32) - cur[:, sl]
        y = _dot(pooled.astype(bf16), pw_ref[g]) * ps_ref[:, sl]
        o_ref[0, :, sl] = y.astype(o_ref.dtype)


def _pool(a, pool_w, pool_scale, tt):
    b, s, w = a.shape
    return pl.pallas_call(
        functools.partial(_pool_kernel, s, tt),
        grid=(b, s // tt),
        in_specs=[
            pl.BlockSpec((1, s, w), lambda i, j: (i, 0, 0)),
            _const_spec(pool_w.shape),
            _const_spec((1, w)),
        ],
        out_specs=pl.BlockSpec((1, tt, w), lambda i, j: (i, j, 0)),
        out_shape=jax.ShapeDtypeStruct((b, s, w), bf16),
        compiler_params=_params(("parallel", "arbitrary"), 40),
        name="pool_mixer",
    )(a, pool_w, pool_scale.reshape(1, w))


def _half_masks():
    lane = lax.broadcasted_iota(jnp.int32, (1, LANES), 1)
    lo = lane < HEAD_DIM
    return lo, jnp.logical_not(lo)


def _attend(qm, keys, vals, biases, extra_logit):
    scale = HEAD_DIM ** -0.5
    logits = []
    for k, bias in zip(keys, biases):
        s = _dot_nt(qm, k) * scale
        if bias is not None:
            s = s + bias
        logits.append(s)
    m = functools.reduce(jnp.maximum, [jnp.max(s, axis=-1, keepdims=True) for s in logits])
    if extra_logit is not None:
        m = jnp.maximum(m, extra_logit)
    den = None
    out = None
    for s, v in zip(logits, vals):
        e = jnp.exp(s - m)
        d = jnp.sum(e, axis=-1, keepdims=True)
        o = _dot(e.astype(bf16), v)
        den = d if den is None else den + d
        out = o if out is None else out + o
    if extra_logit is not None:
        den = den + jnp.exp(extra_logit - m)
    return out / den


NA_ROWS = 2
NA_SPAN = NA_KH + NA_ROWS - 1
NA_CLASSES = 5


def _na_class_rows(rows):
    return (0, 2, 4, rows - 4, rows - 2)


def _na_kernel(rows, q_ref, k_ref, v_ref, kc_ref, vc_ref, bias_ref, o_ref):
    r0 = pl.program_id(1) * NA_ROWS
    k0 = pl.multiple_of(jnp.clip(r0 - NA_KH // 2, 0, rows - NA_SPAN) * GRID_W, GRID_W)
    nk = NA_SPAN * GRID_W
    nq = NA_ROWS * GRID_W
    lo, hi = _half_masks()
    for p in range(NA_HEADS // 2):
        sl = slice(p * LANES, (p + 1) * LANES)
        qp = q_ref[0, :, sl]
        zero = jnp.zeros_like(qp)
        qs = jnp.concatenate([jnp.where(lo, qp, zero), jnp.where(hi, qp, zero)], axis=0)
        bias = jnp.concatenate([bias_ref[2 * p, 0], bias_ref[2 * p + 1, 0]], axis=0)
        o = _attend(qs, (k_ref[0, pl.ds(k0, nk), sl], kc_ref[0, :, sl]),
                    (v_ref[0, pl.ds(k0, nk), sl], vc_ref[0, :, sl]), (bias, None), None)
        o_ref[0, :, sl] = jnp.where(lo, o[:nq], o[nq:]).astype(o_ref.dtype)


def _na(q, knvn, knvn_c, bias_tbl):
    b, s, _ = q.shape
    rows = s // GRID_W
    lc = knvn_c.shape[1]
    nq = NA_ROWS * GRID_W

    def cls(i):
        r0 = i * NA_ROWS
        return jnp.where(r0 < 4, r0 // 2, jnp.where(r0 >= rows - 4, 3 + (r0 - (rows - 4)) // 2, 2))

    return pl.pallas_call(
        functools.partial(_na_kernel, rows),
        grid=(b, rows // NA_ROWS),
        in_specs=[
            pl.BlockSpec((1, nq, NA_WIDTH), lambda i, r: (i, r, 0)),
            pl.BlockSpec((1, s, NA_WIDTH), lambda i, r: (i, 0, 0)),
            pl.BlockSpec((1, s, NA_WIDTH), lambda i, r: (i, 0, 1)),
            pl.BlockSpec((1, lc, NA_WIDTH), lambda i, r: (i, 0, 0)),
            pl.BlockSpec((1, lc, NA_WIDTH), lambda i, r: (i, 0, 1)),
            pl.BlockSpec((NA_HEADS, 1, nq, NA_SPAN * GRID_W), lambda i, r: (0, cls(r), 0, 0)),
        ],
        out_specs=pl.BlockSpec((1, nq, NA_WIDTH), lambda i, r: (i, r, 0)),
        out_shape=jax.ShapeDtypeStruct((b, s, NA_WIDTH), bf16),
        compiler_params=_params(("parallel", "arbitrary"), 48),
        name="nbr_attn",
    )(q, knvn, knvn, knvn_c, knvn_c, bias_tbl)


def _swa_kernel(seq, q_ref, sw_ref, swc_ref, sink_ref, o_ref):
    n = pl.program_id(1)
    span = 3 * SWA_BLOCK
    start = pl.multiple_of(jnp.clip((n - 1) * SWA_BLOCK, 0, seq - span), SWA_BLOCK)
    qpos = n * SWA_BLOCK + lax.broadcasted_iota(jnp.int32, (SWA_BLOCK, 1), 0)
    kpos = start + lax.broadcasted_iota(jnp.int32, (1, span), 1)
    bias1 = jnp.where(jnp.abs(qpos - kpos) <= SWA_WINDOW, 0.0, NEG_BIG)
    lo, hi = _half_masks()
    rep = SWA_HEADS // SWA_KV_HEADS
    bias = jnp.concatenate([bias1] * rep, axis=0)
    for g in range(SWA_KV_HEADS):
        kd = sw_ref[0, pl.ds(start, span), g * LANES:(g + 1) * LANES]
        vd = sw_ref[0, pl.ds(start, span), (2 + g) * LANES:(3 + g) * LANES]
        kcd = swc_ref[0, :, g * LANES:(g + 1) * LANES]
        vcd = swc_ref[0, :, (2 + g) * LANES:(3 + g) * LANES]
        qs, sinks = [], []
        for h in range(g * rep, (g + 1) * rep):
            qp = q_ref[0, :, (h // 2) * LANES:(h // 2 + 1) * LANES]
            qs.append(jnp.where(hi if h % 2 else lo, qp, jnp.zeros_like(qp)))
            sinks.append(jnp.full((SWA_BLOCK, 1), sink_ref[h], f32))
        o = _attend(jnp.concatenate(qs, axis=0), (kd, kcd), (vd, vcd), (bias, None), jnp.concatenate(sinks, axis=0))
        for j in range(rep // 2):
            p = (g * rep) // 2 + j
            o_ref[0, :, p * LANES:(p + 1) * LANES] = jnp.where(
                lo, o[2 * j * SWA_BLOCK:(2 * j + 1) * SWA_BLOCK], o[(2 * j + 1) * SWA_BLOCK:(2 * j + 2) * SWA_BLOCK]
            ).astype(o_ref.dtype)


def _swa(q, sw, sw_c, sink):
    b, s, _ = q.shape
    lc = sw_c.shape[1]
    return pl.pallas_call(
        functools.partial(_swa_kernel, s),
        grid=(b, s // SWA_BLOCK),
        in_specs=[
            pl.BlockSpec((1, SWA_BLOCK, SWA_Q_WIDTH), lambda i, n: (i, n, 1)),
            pl.BlockSpec((1, s, 512), lambda i, n: (i, 0, 0)),
            pl.BlockSpec((1, lc, 512), lambda i, n: (i, 0, 0)),
            pl.BlockSpec(memory_space=pltpu.SMEM),
        ],
        out_specs=pl.BlockSpec((1, SWA_BLOCK, SWA_Q_WIDTH), lambda i, n: (i, n, 0)),
        out_shape=jax.ShapeDtypeStruct((b, s, SWA_Q_WIDTH), bf16),
        compiler_params=_params(("parallel", "arbitrary"), 40),
        name="win_attn",
    )(q, sw, sw_c, sink)


def _ctx_attn_kernel(q_ref, kv_ref, sw_ref, sink_ref, ob_ref, oc_ref):
    lo, hi = _half_masks()
    rep = SWA_HEADS // SWA_KV_HEADS
    for p in range(NA_HEADS // 2):
        sl = slice(p * LANES, (p + 1) * LANES)
        qp = q_ref[0, :, sl]
        kp = kv_ref[0, :, sl]
        vp = kv_ref[0, :, NA_WIDTH + p * LANES:NA_WIDTH + (p + 1) * LANES]
        halves = [_attend(jnp.where(msk, qp, jnp.zeros_like(qp)), (kp,), (vp,), (None,), None) for msk in (lo, hi)]
        ob_ref[0, :, sl] = jnp.where(lo, halves[0], halves[1]).astype(ob_ref.dtype)
    outs = []
    for h in range(SWA_HEADS):
        g = h // rep
        kd = sw_ref[0, :, g * LANES:(g + 1) * LANES]
        vd = sw_ref[0, :, (2 + g) * LANES:(3 + g) * LANES]
        qp = q_ref[0, :, NA_WIDTH + (h // 2) * LANES:NA_WIDTH + (h // 2 + 1) * LANES]
        qm = jnp.where(hi if h % 2 else lo, qp, jnp.zeros_like(qp))
        outs.append(_attend(qm, (kd,), (vd,), (None,), sink_ref[h]))
    for p in range(SWA_HEADS // 2):
        oc_ref[0, :, p * LANES:(p + 1) * LANES] = jnp.where(lo, outs[2 * p], outs[2 * p + 1]).astype(oc_ref.dtype)


def _ctx_attn(q_c, knvn_c, sw_c, sink):
    b, lc, _ = q_c.shape
    spec = lambda w: pl.BlockSpec((1, lc, w), lambda i: (i, 0, 0))
    return pl.pallas_call(
        _ctx_attn_kernel,
        grid=(b,),
        in_specs=[spec(1024), spec(1024), spec(512), pl.BlockSpec(memory_space=pltpu.SMEM)],
        out_specs=[spec(NA_WIDTH), spec(SWA_Q_WIDTH)],
        out_shape=[jax.ShapeDtypeStruct((b, lc, NA_WIDTH), bf16), jax.ShapeDtypeStruct((b, lc, SWA_Q_WIDTH), bf16)],
        compiler_params=_params(("parallel",), 40),
        name="ctx_attn",
    )(q_c, knvn_c, sw_c, sink)


def _merge_kernel(x_ref, ya_ref, ob_ref, oc_ref, g_ref, gt_ref, wa_ref, wb_ref, wc_ref, wo_ref, o_ref):
    d = D_MODEL
    m = g_ref[0, :, 0:d].astype(f32) * _dot(ya_ref[0], wa_ref[...])
    m = m + g_ref[0, :, d:2 * d].astype(f32) * _dot(ob_ref[0], wb_ref[...])
    m = m + g_ref[0, :, 2 * d:3 * d].astype(f32) * _dot(oc_ref[0], wc_ref[...])
    o_ref[0] = x_ref[0] + gt_ref[0] * _dot(m.astype(bf16), wo_ref[...])


def _merge(x, ya, ob, oc, g, gt, wa, wb, wc, wo, tm):
    b, s, d = x.shape
    tok = lambda w: pl.BlockSpec((1, tm, w), lambda i, j: (i, j, 0))
    return pl.pallas_call(
        _merge_kernel,
        grid=(b, s // tm),
        in_specs=[tok(d), tok(512), tok(512), tok(512), tok(3 * d),
                  pl.BlockSpec((1, 1, d), lambda i, j: (i, 0, 0)),
                  _const_spec(wa.shape), _const_spec(wb.shape), _const_spec(wc.shape), _const_spec(wo.shape)],
        out_specs=tok(d),
        out_shape=jax.ShapeDtypeStruct((b, s, d), f32),
        compiler_params=_params(("parallel", "parallel"), 48),
        name="merge_out",
    )(x, ya, ob, oc, g, gt, wa, wb, wc, wo)


def _topk_rows(s, k):
    n = s.shape[0]
    rid = lax.broadcasted_iota(jnp.int32, s.shape, 0)
    vals, idxs = [], []
    for _ in range(k):
        m = jnp.max(s, axis=0, keepdims=True)
        i = jnp.min(jnp.where(s == m, rid, n), axis=0, keepdims=True)
        vals.append(m)
        idxs.append(i)
        s = jnp.where(rid == i, -jnp.inf, s)
    return jnp.concatenate(vals, axis=0), jnp.concatenate(idxs, axis=0)


def _pick_rows(tab, sel):
    rid = lax.broadcasted_iota(jnp.int32, tab.shape, 0)
    rows = []
    for k in range(sel.shape[0]):
        rows.append(jnp.sum(jnp.where(rid == sel[k:k + 1, :], tab, 0.0), axis=0, keepdims=True))
    return jnp.concatenate(rows, axis=0)


def _route_kernel(x_ref, g_ref, sh_ref, sc_ref, wqh_ref, wql_ref, kh_ref, kl_ref, h2_ref, i_ref, j_ref, gate_ref):
    h2 = _rms_mod(x_ref[0], g_ref[...], sh_ref[0], sc_ref[0])
    h2_ref[0] = h2.astype(bf16)
    hh, hl = _split(h2)
    k = PEER_TOPK
    i_rows, j_rows, g_rows = [], [], []
    for h in range(PEER_HEADS):
        top_s, top_i = [], []
        for p in range(2):
            hp = 2 * h + p
            cs = slice(hp * PEER_NKEYS, (hp + 1) * PEER_NKEYS)
            q = _dot(hh, wqh_ref[:, cs]) + _dot(hh, wql_ref[:, cs]) + _dot(hl, wqh_ref[:, cs])
            qh, ql = _split(q)
            st = _dot_nt(kh_ref[hp], qh) + _dot_nt(kh_ref[hp], ql) + _dot_nt(kl_ref[hp], qh)
            ts, ti = _topk_rows(st, k)
            top_s.append(ts)
            top_i.append(ti.astype(f32))
        blocks = [top_s[0][0:1] + top_s[1][0:8], top_s[0][0:1] + top_s[1][8:16]]
        blocks += [top_s[0][a:a + 1] + top_s[1][0:8] for a in range(1, 8)]
        blocks += [top_s[0][8:16] + top_s[1][0:1]]
        best_s, row = _topk_rows(jnp.concatenate(blocks, axis=0), k)
        blk = lax.shift_right_logical(row, 3)
        sub = lax.bitwise_and(row, 7)
        tail = blk == len(blocks) - 1
        sel_a = jnp.where(tail, sub + 8, jnp.maximum(blk - 1, 0))
        sel_b = jnp.where(tail, 0, jnp.where(blk == 1, sub + 8, sub))
        i_rows.append(_pick_rows(top_i[0], sel_a))
        j_rows.append(_pick_rows(top_i[1], sel_b))
        e = jnp.exp(best_s - best_s[0:1, :])
        g_rows.append(e / jnp.sum(e, axis=0, keepdims=True))
    i_ref[0] = jnp.concatenate(i_rows, axis=0).T
    j_ref[0] = jnp.concatenate(j_rows, axis=0).T
    gate_ref[0] = jnp.concatenate(g_rows, axis=0).T


def _route(x, g, sh, sc, wqh, wql, kh, kl, tt):
    b, s, d = x.shape
    tok = lambda w: pl.BlockSpec((1, tt, w), lambda i, j: (i, j, 0))
    return pl.pallas_call(
        _route_kernel,
        grid=(b, s // tt),
        in_specs=[tok(d), _const_spec((1, d)),
                  pl.BlockSpec((1, 1, d), lambda i, j: (i, 0, 0)),
                  pl.BlockSpec((1, 1, d), lambda i, j: (i, 0, 0)),
                  _const_spec(wqh.shape), _const_spec(wql.shape), _const_spec(kh.shape), _const_spec(kl.shape)],
        out_specs=[tok(d), tok(PEER_HK), tok(PEER_HK), tok(PEER_HK)],
        out_shape=[jax.ShapeDtypeStruct((b, s, d), bf16)] + [jax.ShapeDtypeStruct((b, s, PEER_HK), f32)] * 3,
        compiler_params=_params(("parallel", "parallel"), 48),
        name="peer_route",
    )(x, g.reshape(1, d), sh, sc, wqh, wql, kh, kl)


EXPERT_BLOCK = 1024
I_PER_BLOCK = EXPERT_BLOCK // PEER_NKEYS
N_EXPERT_BLOCKS = PEER_NKEYS * PEER_NKEYS // EXPERT_BLOCK
G_PHASES = 2
I_PER_PHASE = PEER_NKEYS // G_PHASES
BLOCKS_PER_PHASE = N_EXPERT_BLOCKS // G_PHASES
G_PITCH_PAD = SUBLANES
BUILD_UNROLL = 16


def _gelu(a):
    return 0.5 * a * (1.0 + lax.erf(a * (2.0 ** -0.5)))


def _expert_kernel(tt, final, x_ref, h2_ref, i_ref, j_ref, gate_ref, gt_ref, ut_ref, v_ref, fg_ref, o_ref, g_buf, acc_ref):
    ib = pl.program_id(2)
    pitch = tt + G_PITCH_PAD
    in_phase = lax.rem(ib, BLOCKS_PER_PHASE)

    @pl.when(in_phase == 0)
    def _build():
        first = lax.div(ib, BLOCKS_PER_PHASE) * I_PER_PHASE
        sub_i = (lax.broadcasted_iota(jnp.int32, (I_PER_PHASE, PEER_HK), 0) + first).astype(f32).astype(bf16)
        sub_j = lax.broadcasted_iota(jnp.int32, (PEER_NKEYS, PEER_HK), 0).astype(f32).astype(bf16)
        one = jnp.ones((), bf16)
        zero = jnp.zeros((), bf16)

        def body(step, carry):
            for u in range(BUILD_UNROLL):
                t = step * BUILD_UNROLL + u
                irow = i_ref[0, pl.ds(t, 1), :].astype(bf16)
                jrow = j_ref[0, pl.ds(t, 1), :].astype(bf16)
                grow = gate_ref[0, pl.ds(t, 1), :].astype(bf16)
                pt = jnp.where(sub_i == irow, one, zero)
                qt = jnp.where(sub_j == jrow, grow, zero)
                g_buf[pl.ds(t, I_PER_PHASE, stride=pitch), :] = _dot_nt(pt, qt)
            return carry

        lax.fori_loop(0, tt // BUILD_UNROLL, body, 0)

    @pl.when(ib == 0)
    def _zero():
        acc_ref[...] = jnp.zeros_like(acc_ref)

    a = _dot(h2_ref[0], ut_ref[...])
    pieces = []
    for k in range(I_PER_BLOCK):
        row0 = pl.multiple_of((in_phase * I_PER_BLOCK + k) * pitch, SUBLANES)
        gk = g_buf[pl.ds(row0, tt), :]
        pieces.append((gk * _gelu(a[:, k * PEER_NKEYS:(k + 1) * PEER_NKEYS])).astype(bf16))
    acc_ref[...] += _dot(jnp.concatenate(pieces, axis=1), v_ref[...])

    @pl.when(ib == N_EXPERT_BLOCKS - 1)
    def _finish():
        y = x_ref[0] + gt_ref[0] * acc_ref[...]
        if final:
            ms = jnp.mean(y * y, axis=-1, keepdims=True)
            y = y * lax.rsqrt(ms + NORM_EPS) * fg_ref[...]
        o_ref[0] = y


def _experts(x, h2, idx_i, idx_j, gate, gt, ut, v, final_g, final, tt):
    b, s, d = x.shape
    tok = lambda w: pl.BlockSpec((1, tt, w), lambda i, j, e: (i, j, 0))
    return pl.pallas_call(
        functools.partial(_expert_kernel, tt, final),
        grid=(b, s // tt, N_EXPERT_BLOCKS),
        in_specs=[tok(d), tok(d), tok(PEER_HK), tok(PEER_HK), tok(PEER_HK),
                  pl.BlockSpec((1, 1, d), lambda i, j, e: (i, 0, 0)),
                  pl.BlockSpec((d, EXPERT_BLOCK), lambda i, j, e: (0, e)),
                  pl.BlockSpec((EXPERT_BLOCK, d), lambda i, j, e: (e, 0)),
                  pl.BlockSpec((1, d), lambda i, j, e: (0, 0))],
        out_specs=tok(d),
        out_shape=jax.ShapeDtypeStruct((b, s, d), f32),
        scratch_shapes=[pltpu.VMEM((I_PER_PHASE * (tt + G_PITCH_PAD), PEER_NKEYS), f32),
                        pltpu.VMEM((tt, d), f32)],
        compiler_params=_params(("parallel", "parallel", "arbitrary"), 56),
        name="peer_experts",
    )(x, h2, idx_i, idx_j, gate, gt, ut, v, final_g.reshape(1, d))


def _rope_tables(seq):
    t = jnp.arange(seq)
    pos = jnp.stack([t // GRID_W, t % GRID_W], axis=1).astype(f32)
    f = HEAD_DIM // 4
    inv = ROPE_BASE ** (-jnp.arange(f, dtype=f32) / f)
    lane = jnp.arange(LANES)
    l64 = lane % HEAD_DIM
    ang = pos[:, (l64 // (2 * f))] * inv[l64 % f][None, :]
    first = (l64 % (2 * f)) < f
    cos = jnp.cos(ang)
    sin = jnp.sin(ang)
    return cos, jnp.where(first[None, :], -sin, 0.0), jnp.where(first[None, :], 0.0, sin)


def _na_bias_table(rpb, rows):
    r0 = np.array(_na_class_rows(rows))[:, None, None]
    a = np.arange(NA_ROWS)[None, :, None]
    kr = np.arange(NA_SPAN)[None, None, :]
    r = r0 + a
    ws = np.clip(r - NA_KH // 2, 0, rows - NA_KH)
    absrow = np.clip(r0 - NA_KH // 2, 0, rows - NA_SPAN) + kr
    ok_r = (absrow >= ws) & (absrow < ws + NA_KH)
    dr = absrow - r + NA_KH - 1
    oh_r = (ok_r[..., None] & (dr[..., None] == np.arange(2 * NA_KH - 1))).astype(np.float32)
    qc = np.arange(GRID_W)[:, None]
    kc = np.arange(GRID_W)[None, :]
    cstart = np.clip(qc - NA_KW // 2, 0, GRID_W - NA_KW)
    ok_c = (kc >= cstart) & (kc < cstart + NA_KW)
    dc = kc - qc + NA_KW - 1
    oh_c = (ok_c[..., None] & (dc[..., None] == np.arange(2 * NA_KW - 1))).astype(np.float32)
    hp = lax.Precision.HIGHEST
    t1 = jnp.einsum("hde,qke->hdqk", rpb.astype(f32), jnp.asarray(oh_c), precision=hp)
    bias = jnp.einsum("cafd,hdqk->hcaqfk", jnp.asarray(oh_r), t1, precision=hp)
    ok = ok_r[None, :, :, None, :, None] & ok_c[None, None, None, :, None, :]
    bias = jnp.where(jnp.asarray(ok), bias, NEG_BIG)
    return bias.reshape(NA_HEADS, NA_CLASSES, NA_ROWS * GRID_W, NA_SPAN * GRID_W)


def _layout_w_in(w):
    c = lambda off, width: w[:, off:off + width]
    hd = HEAD_DIM
    k0, k1 = c(OFF_KW, hd), c(OFF_KW + hd, hd)
    v0, v1 = c(OFF_VW, hd), c(OFF_VW + hd, hd)
    cols = [c(OFF_KN, NA_WIDTH), c(OFF_VN, NA_WIDTH), k0, k0, k1, k1, v0, v0, v1, v1,
            c(OFF_QN, NA_WIDTH), c(OFF_QW, SWA_Q_WIDTH), c(OFF_A, POOL_WIDTH), c(OFF_G, 3 * D_MODEL)]
    return jnp.concatenate(cols, axis=1).astype(bf16)


def _bsplit(a):
    hi = a.astype(bf16)
    return hi, (a - hi.astype(f32)).astype(bf16)


LATENT_TILE = 512
ROUTE_TILE = 256
EXPERT_TILE = 512


def kernel(x, c, ctx, c_ctx, ada_w, ada_b, norm1_g, norm2_g, w_in, pool_w, pool_scale, na_rpb, swa_sink,
           w_branch_a, w_branch_b, w_branch_c, w_out, peer_wq, peer_keys, peer_u, peer_v, final_g):
    b, s, d = x.shape
    lc = ctx.shape[1]
    depth = ada_w.shape[0]
    cpad = jnp.concatenate([c, c_ctx[None, :], jnp.zeros((SUBLANES - b - 1, d), f32)], axis=0)
    mod = _modulation(cpad, ada_w, ada_b)
    rope_tabs = _rope_tables(s)
    xc = ctx
    for l in range(depth):
        last = l == depth - 1
        ml = mod[l, :b].reshape(b, 6, 1, d)
        sh1, sc1, gt1, sh2, sc2, gt2 = (ml[:, i] for i in range(6))
        mc = jnp.broadcast_to(mod[l, b].reshape(1, 6, 1, d), (b, 6, 1, d))
        csh1, csc1, cgt1, csh2, csc2, cgt2 = (mc[:, i] for i in range(6))

        w_aug = _layout_w_in(w_in[l])
        wa, wb, wc, wo = (w[l].astype(bf16) for w in (w_branch_a, w_branch_b, w_branch_c, w_out))
        pw = pool_w[l].astype(bf16)
        bias_tbl = _na_bias_table(na_rpb[l], s // GRID_W)
        wqh, wql = _bsplit(peer_wq[l])
        kh, kl = _bsplit(peer_keys[l].reshape(2 * PEER_HEADS, PEER_NKEYS, -1))
        ut = peer_u[l].T.astype(bf16)
        vb = peer_v[l].astype(bf16)

        knvn, sw, q, a, g = _inproj(x, norm1_g[l], sh1, sc1, w_aug, SEGS_ALL, rope_tabs, LATENT_TILE)
        if last:
            knvn_c, sw_c = _inproj(xc, norm1_g[l], csh1, csc1, w_aug, SEGS_KV, None, lc)
        else:
            knvn_c, sw_c, q_c, a_c, g_c = _inproj(xc, norm1_g[l], csh1, csc1, w_aug, SEGS_ALL, None, lc)
        ya = _pool(a, pw, pool_scale[l], 256)
        ob = _na(q, knvn, knvn_c, bias_tbl)
        oc = _swa(q, sw, sw_c, swa_sink[l])
        x = _merge(x, ya, ob, oc, g, gt1, wa, wb, wc, wo, LATENT_TILE)
        if not last:
            ya_c = _pool(a_c, pw, pool_scale[l], lc)
            ob_c, oc_c = _ctx_attn(q_c, knvn_c, sw_c, swa_sink[l])
            xc = _merge(xc, ya_c, ob_c, oc_c, g_c, cgt1, wa, wb, wc, wo, lc)

        h2, ii, jj, gate = _route(x, norm2_g[l], sh2, sc2, wqh, wql, kh, kl, ROUTE_TILE)
        x = _experts(x, h2, ii, jj, gate, gt2, ut, vb, final_g, last, EXPERT_TILE)
        if not last:
            xf = xc.reshape(1, b * lc, d)
            h2c, iic, jjc, gatec = _route(xf, norm2_g[l], csh2[:1], csc2[:1], wqh, wql, kh, kl, ROUTE_TILE)
            xc = _experts(xf, h2c, iic, jjc, gatec, cgt2[:1], ut, vb, final_g, False, EXPERT_TILE).reshape(b, lc, d)
    return x
```

```python
import functools

import jax
import jax.numpy as jnp
import numpy as np
from jax import lax
from jax.experimental import pallas as pl
from jax.experimental.pallas import tpu as pltpu

f32 = jnp.float32
bf16 = jnp.bfloat16

D_MODEL = 1024
GRID_W = 64
HEAD_DIM = 64
NORM_EPS = 1e-6
ROPE_BASE = 10000.0

POOL_WIDTH = 512
POOL_WINDOWS = (2, 4, 8, 16)
POOL_GROUP_W = 128
POOL_HALO = 8

NA_HEADS = 8
NA_KH = 8
NA_KW = 16

SWA_HEADS = 8
SWA_KV_HEADS = 2
SWA_WINDOW = 128
SWA_BLOCK = 128

PEER_HEADS = 8
PEER_NKEYS = 128
PEER_TOPK = 16
PEER_HK = PEER_HEADS * PEER_TOPK

NA_WIDTH = NA_HEADS * HEAD_DIM
SWA_Q_WIDTH = SWA_HEADS * HEAD_DIM
SWA_KV_WIDTH = SWA_KV_HEADS * HEAD_DIM
OFF_KN = 0
OFF_VN = OFF_KN + NA_WIDTH
OFF_KW = OFF_VN + NA_WIDTH
OFF_VW = OFF_KW + SWA_KV_WIDTH
OFF_QN = OFF_VW + SWA_KV_WIDTH
OFF_QW = OFF_QN + NA_WIDTH
OFF_A = OFF_QW + SWA_Q_WIDTH
OFF_G = OFF_A + POOL_WIDTH

LANES = 128
SUBLANES = 8
NEG_BIG = -1e30
MIB = 1024 * 1024

SEG_KN = ("knvn", 0, 1024)
SEG_SW = ("sw", 1024, 512)
SEG_Q = ("q", 1536, 1024)
SEG_A = ("a", 2560, 512)
SEG_G = ("g", 3072, 3072)
SEGS_ALL = (SEG_KN, SEG_SW, SEG_Q, SEG_A, SEG_G)
SEGS_KV = (SEG_KN, SEG_SW)
SEG_DTYPE = {"knvn": bf16, "sw": bf16, "q": bf16, "a": f32, "g": bf16}
COL_CHUNK = 512


def _params(sem, vmem_mib):
    return pltpu.CompilerParams(dimension_semantics=sem, vmem_limit_bytes=vmem_mib * MIB)


def _const_spec(shape):
    nd = len(shape)
    return pl.BlockSpec(shape, lambda *_: (0,) * nd, pipeline_mode=pl.Buffered(1))


def _dot(a, b):
    return jnp.dot(a, b, preferred_element_type=f32)


def _dot_nt(a, b):
    return lax.dot_general(a, b, (((1,), (1,)), ((), ())), preferred_element_type=f32)


def _split(a):
    hi = a.astype(bf16)
    lo = (a - hi.astype(f32)).astype(bf16)
    return hi, lo


def _sigmoid(z):
    return 1.0 / (1.0 + jnp.exp(-z))


def _rms_mod(x, g, sh, sc):
    ms = jnp.mean(x * x, axis=-1, keepdims=True)
    h = x * lax.rsqrt(ms + NORM_EPS) * g
    return h * (1.0 + sc) + sh


def _mod_kernel(c_ref, w_ref, b_ref, o_ref):
    c = c_ref[...]
    s = c * _sigmoid(c)
    sh, sl = _split(s)
    wh, wl = _split(w_ref[0])
    o_ref[0] = _dot(sh, wh) + _dot(sh, wl) + _dot(sl, wh) + b_ref[0]


def _modulation(cpad, ada_w, ada_b):
    depth, d, n = ada_w.shape
    tn = 1536
    return pl.pallas_call(
        _mod_kernel,
        grid=(depth, n // tn),
        in_specs=[
            pl.BlockSpec((SUBLANES, d), lambda l, j: (0, 0)),
            pl.BlockSpec((1, d, tn), lambda l, j: (l, 0, j)),
            pl.BlockSpec((1, 1, tn), lambda l, j: (l, 0, j)),
        ],
        out_specs=pl.BlockSpec((1, SUBLANES, tn), lambda l, j: (l, 0, j)),
        out_shape=jax.ShapeDtypeStruct((depth, SUBLANES, n), f32),
        compiler_params=_params(("arbitrary", "arbitrary"), 40),
        name="adaln_mod",
    )(cpad, ada_w, ada_b.reshape(depth, 1, n))


def _rope(z, cos, s1, s2):
    return z * cos + pltpu.roll(z, LANES - 16, 1) * s1 + pltpu.roll(z, 16, 1) * s2


def _inproj_kernel(segs, rope, x_ref, g_ref, sh_ref, sc_ref, w_ref, *rest):
    if rope:
        cos_ref, s1_ref, s2_ref = rest[:3]
        outs = rest[3:]
        cos, s1, s2 = cos_ref[...], s1_ref[...], s2_ref[...]
    else:
        outs = rest
    hb = _rms_mod(x_ref[0], g_ref[...], sh_ref[0], sc_ref[0]).astype(bf16)
    for (name, c0, width), o_ref in zip(segs, outs):
        for j in range(0, width, COL_CHUNK):
            z = _dot(hb, w_ref[:, c0 + j:c0 + j + COL_CHUNK])
            if name == "g":
                z = _sigmoid(z)
            if rope and name in ("sw", "q"):
                n_rot = 2 if name == "sw" else (4 if j == COL_CHUNK else 0)
                slabs = [z[:, s * LANES:(s + 1) * LANES] for s in range(COL_CHUNK // LANES)]
                slabs = [_rope(sl, cos, s1, s2) if s < n_rot else sl for s, sl in enumerate(slabs)]
                z = jnp.concatenate(slabs, axis=1)
            o_ref[0, :, j:j + COL_CHUNK] = z.astype(o_ref.dtype)


def _inproj(x, g, sh, sc, w_aug, segs, rope_tabs, tm):
    b, s, d = x.shape
    rope = rope_tabs is not None
    in_specs = [
        pl.BlockSpec((1, tm, d), lambda i, j: (i, j, 0)),
        _const_spec((1, d)),
        pl.BlockSpec((1, 1, d), lambda i, j: (i, 0, 0)),
        pl.BlockSpec((1, 1, d), lambda i, j: (i, 0, 0)),
        _const_spec(w_aug.shape),
    ]
    args = [x, g.reshape(1, d), sh, sc, w_aug]
    if rope:
        in_specs += [pl.BlockSpec((tm, LANES), lambda i, j: (j, 0))] * 3
        args += list(rope_tabs)
    out_specs = [pl.BlockSpec((1, tm, w), lambda i, j: (i, j, 0)) for _, _, w in segs]
    out_shape = [jax.ShapeDtypeStruct((b, s, w), SEG_DTYPE[n]) for n, _, w in segs]
    return pl.pallas_call(
        functools.partial(_inproj_kernel, segs, rope),
        grid=(b, s // tm),
        in_specs=in_specs,
        out_specs=out_specs,
        out_shape=out_shape,
        compiler_params=_params(("parallel", "parallel"), 52),
        name="inproj",
    )(*args)


def _pool_kernel(seq, tt, a_ref, pw_ref, ps_ref, o_ref):
    t0 = pl.multiple_of(pl.program_id(1) * tt, tt)
    cur = a_ref[0, pl.ds(t0, tt), :]
    p0 = pl.multiple_of(jnp.maximum(t0 - POOL_HALO, 0), POOL_HALO)
    n0 = pl.multiple_of(jnp.minimum(t0 + tt, seq - POOL_HALO), POOL_HALO)
    prev = jnp.where(t0 > 0, a_ref[0, pl.ds(p0, POOL_HALO), :], 0.0)
    nxt = jnp.where(t0 + tt < seq, a_ref[0, pl.ds(n0, POOL_HALO), :], 0.0)
    ext = jnp.concatenate([prev, cur, nxt], axis=0)
    n_ext = tt + 2 * POOL_HALO
    tpos = t0 + lax.broadcasted_iota(jnp.int32, (tt, 1), 0)
    for g, w in enumerate(POOL_WINDOWS):
        sl = slice(g * POOL_GROUP_W, (g + 1) * POOL_GROUP_W)
        eg = ext[:, sl]
        acc = jnp.zeros((tt, POOL_GROUP_W), f32)
        for k in range(-(w // 2), w - w // 2):
            shifted = eg if k == 0 else pltpu.roll(eg, (-k) % n_ext, 0)
            acc = acc + shifted[POOL_HALO:POOL_HALO + tt]
        lo = jnp.clip(tpos - w // 2, 0, seq)
        hi = jnp.clip(tpos - w // 2 + w, 0, seq)
        pooled = acc / (hi - lo).astype(f32) - cur[:, sl]
        y = _dot(pooled.astype(bf16), pw_ref[g]) * ps_ref[:, sl]
        o_ref[0, :, sl] = y.astype(o_ref.dtype)


def _pool(a, pool_w, pool_scale, tt):
    b, s, w = a.shape
    return pl.pallas_call(
        functools.partial(_pool_kernel, s, tt),
        grid=(b, s // tt),
        in_specs=[
            pl.BlockSpec((1, s, w), lambda i, j: (i, 0, 0)),
            _const_spec(pool_w.shape),
            _const_spec((1, w)),
        ],
        out_specs=pl.BlockSpec((1, tt, w), lambda i, j: (i, j, 0)),
        out_shape=jax.ShapeDtypeStruct((b, s, w), bf16),
        compiler_params=_params(("parallel", "arbitrary"), 40),
        name="pool_mixer",
    )(a, pool_w, pool_scale.reshape(1, w))


def _half_masks():
    lane = lax.broadcasted_iota(jnp.int32, (1, LANES), 1)
    lo = lane < HEAD_DIM
    return lo, jnp.logical_not(lo)


def _attend_separate(logits, vals, extra_logit):
    m = functools.reduce(jnp.maximum, [jnp.max(s, axis=-1, keepdims=True) for s in logits])
    if extra_logit is not None:
        m = jnp.maximum(m, extra_logit)
    den = None
    out = None
    for s, v in zip(logits, vals):
        e = jnp.exp(s - m)
        d = jnp.sum(e, axis=-1, keepdims=True)
        o = _dot(e.astype(bf16), v)
        den = d if den is None else den + d
        out = o if out is None else out + o
    if extra_logit is not None:
        den = den + jnp.exp(extra_logit - m)
    return out / den


def _attend(qm, keys, vals, biases, extra_logit):
    scale = HEAD_DIM ** -0.5
    logits = []
    for k, bias in zip(keys, biases):
        s = _dot_nt(qm, k) * scale
        if bias is not None:
            s = s + bias
        logits.append(s)
    if any(k.shape[0] % LANES for k in keys):
        return _attend_separate(logits, vals, extra_logit)
    s = jnp.concatenate(logits, axis=-1)
    m = jnp.max(s, axis=-1, keepdims=True)
    if extra_logit is not None:
        m = jnp.maximum(m, extra_logit)
    e = jnp.exp(s - m)
    den = jnp.sum(e, axis=-1, keepdims=True)
    if extra_logit is not None:
        den = den + jnp.exp(extra_logit - m)
    eb = e.astype(bf16)
    out = None
    c0 = 0
    for k, v in zip(keys, vals):
        o = _dot(eb[:, c0:c0 + k.shape[0]], v)
        out = o if out is None else out + o
        c0 += k.shape[0]
    return out / den


NA_ROWS = 2
NA_SPAN = NA_KH + NA_ROWS - 1
NA_CLASSES = 5


def _na_class_rows(rows):
    return (0, 2, 4, rows - 4, rows - 2)


def _na_kernel(rows, q_ref, k_ref, v_ref, kc_ref, vc_ref, bias_ref, o_ref):
    r0 = pl.program_id(1) * NA_ROWS
    k0 = pl.multiple_of(jnp.clip(r0 - NA_KH // 2, 0, rows - NA_SPAN) * GRID_W, GRID_W)
    nk = NA_SPAN * GRID_W
    nq = NA_ROWS * GRID_W
    lo, hi = _half_masks()
    for p in range(NA_HEADS // 2):
        sl = slice(p * LANES, (p + 1) * LANES)
        qp = q_ref[0, :, sl]
        zero = jnp.zeros_like(qp)
        qs = jnp.concatenate([jnp.where(lo, qp, zero), jnp.where(hi, qp, zero)], axis=0)
        bias = jnp.concatenate([bias_ref[2 * p, 0], bias_ref[2 * p + 1, 0]], axis=0)
        o = _attend(qs, (kc_ref[0, :, sl], k_ref[0, pl.ds(k0, nk), sl]),
                    (vc_ref[0, :, sl], v_ref[0, pl.ds(k0, nk), sl]), (None, bias), None)
        o_ref[0, :, sl] = jnp.where(lo, o[:nq], o[nq:]).astype(o_ref.dtype)


def _na(q, knvn, knvn_c, bias_tbl):
    b, s, _ = q.shape
    rows = s // GRID_W
    lc = knvn_c.shape[1]
    nq = NA_ROWS * GRID_W

    def cls(i):
        r0 = i * NA_ROWS
        return jnp.where(r0 < 4, r0 // 2, jnp.where(r0 >= rows - 4, 3 + (r0 - (rows - 4)) // 2, 2))

    return pl.pallas_call(
        functools.partial(_na_kernel, rows),
        grid=(b, rows // NA_ROWS),
        in_specs=[
            pl.BlockSpec((1, nq, NA_WIDTH), lambda i, r: (i, r, 0)),
            pl.BlockSpec((1, s, NA_WIDTH), lambda i, r: (i, 0, 0)),
            pl.BlockSpec((1, s, NA_WIDTH), lambda i, r: (i, 0, 1)),
            pl.BlockSpec((1, lc, NA_WIDTH), lambda i, r: (i, 0, 0)),
            pl.BlockSpec((1, lc, NA_WIDTH), lambda i, r: (i, 0, 1)),
            pl.BlockSpec((NA_HEADS, 1, nq, NA_SPAN * GRID_W), lambda i, r: (0, cls(r), 0, 0)),
        ],
        out_specs=pl.BlockSpec((1, nq, NA_WIDTH), lambda i, r: (i, r, 0)),
        out_shape=jax.ShapeDtypeStruct((b, s, NA_WIDTH), bf16),
        compiler_params=_params(("parallel", "arbitrary"), 48),
        name="nbr_attn",
    )(q, knvn, knvn, knvn_c, knvn_c, bias_tbl)


def _swa_kernel(seq, q_ref, sw_ref, swc_ref, sink_ref, o_ref):
    n = pl.program_id(1)
    span = 3 * SWA_BLOCK
    start = pl.multiple_of(jnp.clip((n - 1) * SWA_BLOCK, 0, seq - span), SWA_BLOCK)
    qpos = n * SWA_BLOCK + lax.broadcasted_iota(jnp.int32, (SWA_BLOCK, 1), 0)
    kpos = start + lax.broadcasted_iota(jnp.int32, (1, span), 1)
    bias1 = jnp.where(jnp.abs(qpos - kpos) <= SWA_WINDOW, 0.0, NEG_BIG)
    lo, hi = _half_masks()
    rep = SWA_HEADS // SWA_KV_HEADS
    bias = jnp.concatenate([bias1] * rep, axis=0)
    for g in range(SWA_KV_HEADS):
        kd = sw_ref[0, pl.ds(start, span), g * LANES:(g + 1) * LANES]
        vd = sw_ref[0, pl.ds(start, span), (2 + g) * LANES:(3 + g) * LANES]
        kcd = swc_ref[0, :, g * LANES:(g + 1) * LANES]
        vcd = swc_ref[0, :, (2 + g) * LANES:(3 + g) * LANES]
        qs, sinks = [], []
        for h in range(g * rep, (g + 1) * rep):
            qp = q_ref[0, :, (h // 2) * LANES:(h // 2 + 1) * LANES]
            qs.append(jnp.where(hi if h % 2 else lo, qp, jnp.zeros_like(qp)))
            sinks.append(jnp.full((SWA_BLOCK, 1), sink_ref[h], f32))
        o = _attend(jnp.concatenate(qs, axis=0), (kd, kcd), (vd, vcd), (bias, None), jnp.concatenate(sinks, axis=0))
        for j in range(rep // 2):
            p = (g * rep) // 2 + j
            o_ref[0, :, p * LANES:(p + 1) * LANES] = jnp.where(
                lo, o[2 * j * SWA_BLOCK:(2 * j + 1) * SWA_BLOCK], o[(2 * j + 1) * SWA_BLOCK:(2 * j + 2) * SWA_BLOCK]
            ).astype(o_ref.dtype)


def _swa(q, sw, sw_c, sink):
    b, s, _ = q.shape
    lc = sw_c.shape[1]
    return pl.pallas_call(
        functools.partial(_swa_kernel, s),
        grid=(b, s // SWA_BLOCK),
        in_specs=[
            pl.BlockSpec((1, SWA_BLOCK, SWA_Q_WIDTH), lambda i, n: (i, n, 1)),
            pl.BlockSpec((1, s, 512), lambda i, n: (i, 0, 0)),
            pl.BlockSpec((1, lc, 512), lambda i, n: (i, 0, 0)),
            pl.BlockSpec(memory_space=pltpu.SMEM),
        ],
        out_specs=pl.BlockSpec((1, SWA_BLOCK, SWA_Q_WIDTH), lambda i, n: (i, n, 0)),
        out_shape=jax.ShapeDtypeStruct((b, s, SWA_Q_WIDTH), bf16),
        compiler_params=_params(("parallel", "arbitrary"), 40),
        name="win_attn",
    )(q, sw, sw_c, sink)


def _ctx_attn_kernel(q_ref, kv_ref, sw_ref, sink_ref, ob_ref, oc_ref):
    lo, hi = _half_masks()
    rep = SWA_HEADS // SWA_KV_HEADS
    for p in range(NA_HEADS // 2):
        sl = slice(p * LANES, (p + 1) * LANES)
        qp = q_ref[0, :, sl]
        kp = kv_ref[0, :, sl]
        vp = kv_ref[0, :, NA_WIDTH + p * LANES:NA_WIDTH + (p + 1) * LANES]
        halves = [_attend(jnp.where(msk, qp, jnp.zeros_like(qp)), (kp,), (vp,), (None,), None) for msk in (lo, hi)]
        ob_ref[0, :, sl] = jnp.where(lo, halves[0], halves[1]).astype(ob_ref.dtype)
    outs = []
    for h in range(SWA_HEADS):
        g = h // rep
        kd = sw_ref[0, :, g * LANES:(g + 1) * LANES]
        vd = sw_ref[0, :, (2 + g) * LANES:(3 + g) * LANES]
        qp = q_ref[0, :, NA_WIDTH + (h // 2) * LANES:NA_WIDTH + (h // 2 + 1) * LANES]
        qm = jnp.where(hi if h % 2 else lo, qp, jnp.zeros_like(qp))
        outs.append(_attend(qm, (kd,), (vd,), (None,), sink_ref[h]))
    for p in range(SWA_HEADS // 2):
        oc_ref[0, :, p * LANES:(p + 1) * LANES] = jnp.where(lo, outs[2 * p], outs[2 * p + 1]).astype(oc_ref.dtype)


def _ctx_attn(q_c, knvn_c, sw_c, sink):
    b, lc, _ = q_c.shape
    spec = lambda w: pl.BlockSpec((1, lc, w), lambda i: (i, 0, 0))
    return pl.pallas_call(
        _ctx_attn_kernel,
        grid=(b,),
        in_specs=[spec(1024), spec(1024), spec(512), pl.BlockSpec(memory_space=pltpu.SMEM)],
        out_specs=[spec(NA_WIDTH), spec(SWA_Q_WIDTH)],
        out_shape=[jax.ShapeDtypeStruct((b, lc, NA_WIDTH), bf16), jax.ShapeDtypeStruct((b, lc, SWA_Q_WIDTH), bf16)],
        compiler_params=_params(("parallel",), 40),
        name="ctx_attn",
    )(q_c, knvn_c, sw_c, sink)


def _merge_kernel(x_ref, ya_ref, ob_ref, oc_ref, g_ref, gt_ref, wa_ref, wb_ref, wc_ref, wo_ref, o_ref):
    d = D_MODEL
    m = g_ref[0, :, 0:d].astype(f32) * _dot(ya_ref[0], wa_ref[...])
    m = m + g_ref[0, :, d:2 * d].astype(f32) * _dot(ob_ref[0], wb_ref[...])
    m = m + g_ref[0, :, 2 * d:3 * d].astype(f32) * _dot(oc_ref[0], wc_ref[...])
    o_ref[0] = x_ref[0] + gt_ref[0] * _dot(m.astype(bf16), wo_ref[...])


def _merge(x, ya, ob, oc, g, gt, wa, wb, wc, wo, tm):
    b, s, d = x.shape
    tok = lambda w: pl.BlockSpec((1, tm, w), lambda i, j: (i, j, 0))
    return pl.pallas_call(
        _merge_kernel,
        grid=(b, s // tm),
        in_specs=[tok(d), tok(512), tok(512), tok(512), tok(3 * d),
                  pl.BlockSpec((1, 1, d), lambda i, j: (i, 0, 0)),
                  _const_spec(wa.shape), _const_spec(wb.shape), _const_spec(wc.shape), _const_spec(wo.shape)],
        out_specs=tok(d),
        out_shape=jax.ShapeDtypeStruct((b, s, d), f32),
        compiler_params=_params(("parallel", "parallel"), 48),
        name="merge_out",
    )(x, ya, ob, oc, g, gt, wa, wb, wc, wo)


def _topk_rows(s, k):
    n = s.shape[0]
    rid = lax.broadcasted_iota(jnp.int32, s.shape, 0)
    vals, idxs = [], []
    for _ in range(k):
        m = jnp.max(s, axis=0, keepdims=True)
        i = jnp.min(jnp.where(s == m, rid, n), axis=0, keepdims=True)
        vals.append(m)
        idxs.append(i)
        s = jnp.where(rid == i, -jnp.inf, s)
    return jnp.concatenate(vals, axis=0), jnp.concatenate(idxs, axis=0)


def _pick_rows(tab, sel):
    rid = lax.broadcasted_iota(jnp.int32, tab.shape, 0)
    rows = []
    for k in range(sel.shape[0]):
        rows.append(jnp.sum(jnp.where(rid == sel[k:k + 1, :], tab, 0.0), axis=0, keepdims=True))
    return jnp.concatenate(rows, axis=0)


def _route_exact_chunk(h2, wq_ref, kd_ref):
    hb = h2.astype(bf16)
    k = PEER_TOPK
    nk = PEER_NKEYS
    i_rows, j_rows, g_rows = [], [], []
    for h in range(PEER_HEADS):
        top_s, top_i = [], []
        for p in range(2):
            cs = slice(p * nk, (p + 1) * nk)
            q = _dot(hb, wq_ref[h, :, cs])
            st = _dot_nt(kd_ref[h, cs, cs], q.astype(bf16))
            ts, ti = _topk_rows(st, k)
            top_s.append(ts)
            top_i.append(ti.astype(f32))
        blocks = [top_s[0][0:1] + top_s[1][0:8], top_s[0][0:1] + top_s[1][8:16]]
        blocks += [top_s[0][a:a + 1] + top_s[1][0:8] for a in range(1, 8)]
        blocks += [top_s[0][8:16] + top_s[1][0:1]]
        best_s, row = _topk_rows(jnp.concatenate(blocks, axis=0), k)
        blk = lax.shift_right_logical(row, 3)
        sub = lax.bitwise_and(row, 7)
        tail = blk == len(blocks) - 1
        sel_a = jnp.where(tail, sub + 8, jnp.maximum(blk - 1, 0))
        sel_b = jnp.where(tail, 0, jnp.where(blk == 1, sub + 8, sub))
        i_rows.append(_pick_rows(top_i[0], sel_a))
        j_rows.append(_pick_rows(top_i[1], sel_b))
        e = jnp.exp(best_s - best_s[0:1, :])
        g_rows.append(e / jnp.sum(e, axis=0, keepdims=True))
    return (jnp.concatenate(i_rows, axis=0).T, jnp.concatenate(j_rows, axis=0).T, jnp.concatenate(g_rows, axis=0).T)


ROUTE_TILE = SUBLANES * LANES
STAGE_PITCH = PEER_NKEYS + SUBLANES
EXACT_CHUNK = 256


def _batcher_pairs(n):
    pairs, p = [], 1
    while p < n:
        k = p
        while k >= 1:
            for j in range(k % p, n - k, 2 * k):
                for i in range(min(k, n - j - k)):
                    if (i + j) // (2 * p) == (i + j + k) // (2 * p):
                        pairs.append((i + j, i + j + k))
            k //= 2
        p *= 2
    return tuple(pairs)


def _bitonic_pairs(n):
    pairs, j = [], n // 2
    while j >= 1:
        pairs += [(i, i + j) for i in range(n) if (i & j) == 0]
        j //= 2
    return tuple(pairs)


def _swap_by_value(x, y):
    return y[0] > x[0]


def _swap_by_value_then_index(x, y):
    return jnp.logical_or(y[0] > x[0], jnp.logical_and(y[0] == x[0], y[1] < x[1]))


def _cmpx(x, y, swap_fn, want_lo=True):
    swap = swap_fn(x, y)
    hi = tuple(jnp.where(swap, b, a) for a, b in zip(x, y))
    lo = tuple(jnp.where(swap, a, b) for a, b in zip(x, y)) if want_lo else None
    return hi, lo


def _run_net(items, pairs, swap_fn):
    items = list(items)
    for i, j in pairs:
        a, b = items[i], items[j]
        if b is None:
            continue
        if a is None:
            items[i], items[j] = b, None
        else:
            items[i], items[j] = _cmpx(a, b, swap_fn)
    return items


def _merge_all(a, b, swap_fn):
    return _run_net(list(a) + list(b)[::-1], _bitonic_pairs(2 * len(a)), swap_fn)


def _merge_top(a, b, swap_fn, dropped=None):
    m = len(a)
    top = []
    for k in range(m):
        x, y = a[k], b[m - 1 - k]
        if x is None or y is None:
            top.append(y if x is None else x)
        else:
            hi, lo = _cmpx(x, y, swap_fn, want_lo=dropped is not None)
            top.append(hi)
            if dropped is not None:
                dropped[0] = jnp.maximum(dropped[0], lo[0])
    return _run_net(top, _bitonic_pairs(m), swap_fn)


def _top_pairs(pair, neg):
    def row(a, n, width):
        return [pair(a, b) for b in range(n)] + [None] * (width - n)

    dropped = [neg]
    by_value = _swap_by_value
    x1 = _merge_all(row(1, 8, 8), [pair(a, 0) for a in range(8, 16)], by_value)
    x2 = _merge_all(row(2, 5, 8), row(3, 4, 8), by_value)
    x3 = _merge_all(row(4, 3, 4), row(5, 2, 4), by_value)
    x4 = _merge_all(row(6, 2, 2), row(7, 2, 2), by_value)
    x5 = _merge_all(x3, x4 + [None] * 4, by_value)
    y1 = _merge_top(row(0, 16, 16), x1, by_value, dropped)
    y2 = _merge_top(x2, x5, by_value, dropped)
    return _merge_top(y1, y2, by_value, dropped), dropped[0]


def _route_kernel(tt, x_ref, g_ref, sh_ref, sc_ref, wq_ref, kd_ref,
                  h2_ref, i_ref, j_ref, gate_ref,
                  hb_ref, st_ref, lv_ref, li_ref, ts_ref, ti_ref, oi_ref, oj_ref, og_ref, tie_ref):
    k = PEER_TOPK
    nk = PEER_NKEYS
    hb = _rms_mod(x_ref[0], g_ref[...], sh_ref[0], sc_ref[0]).astype(bf16)
    h2_ref[0] = hb
    hb_ref[...] = hb
    neg = jnp.full((SUBLANES, LANES), -jnp.inf, f32)

    def scores(h, slot):
        q = _dot(hb_ref[...], wq_ref[h])
        st = _dot_nt(kd_ref[h], q.astype(bf16))
        for p in range(2):
            for c in range(tt // LANES):
                st_ref[slot, p, c * STAGE_PITCH:c * STAGE_PITCH + nk, :] = (
                    st[p * nk:(p + 1) * nk, c * LANES:(c + 1) * LANES])

    scores(0, 0)

    def head(h, tie):
        slot = lax.rem(h, 2)
        for p in range(2):
            exact = _swap_by_value_then_index
            for grp in range(nk // k):
                items = [(st_ref[slot, p, pl.ds(grp * k + e, SUBLANES, stride=STAGE_PITCH), :],
                          jnp.full((SUBLANES, LANES), float(grp * k + e), f32)) for e in range(k)]
                for e, (v, idx) in enumerate(_run_net(items, _batcher_pairs(k), exact)):
                    lv_ref[grp, e] = v
                    li_ref[grp, e] = idx
            span = 1
            while span < nk // k:
                for ga in range(0, nk // k, 2 * span):
                    gb = ga + span
                    top = _merge_top([(lv_ref[ga, e], li_ref[ga, e]) for e in range(k)],
                                     [(lv_ref[gb, e], li_ref[gb, e]) for e in range(k)], exact)
                    for e, (v, idx) in enumerate(top):
                        if 2 * span < nk // k:
                            lv_ref[ga, e] = v
                            li_ref[ga, e] = idx
                        else:
                            ts_ref[p, e] = v
                            ti_ref[p, e] = idx
                span *= 2

        best, loser = _top_pairs(lambda a, b: (ts_ref[0, a] + ts_ref[1, b], ti_ref[0, a], ti_ref[1, b]), neg)
        tie = jnp.maximum(tie, (best[k - 1][0] == loser).astype(f32))
        es = [jnp.exp(v - best[0][0]) for v, _, _ in best]
        den = functools.reduce(lambda u, w: u + w, es)
        for e in range(k):
            r0 = pl.multiple_of((h * k + e) * SUBLANES, SUBLANES)
            oi_ref[pl.ds(r0, SUBLANES), :] = best[e][1]
            oj_ref[pl.ds(r0, SUBLANES), :] = best[e][2]
            og_ref[pl.ds(r0, SUBLANES), :] = es[e] / den
        scores(jnp.minimum(h + 1, PEER_HEADS - 1), 1 - slot)
        return tie

    tie = lax.fori_loop(0, PEER_HEADS, head, jnp.zeros((SUBLANES, LANES), f32))
    for c in range(tt // LANES):
        rows = slice(c * LANES, (c + 1) * LANES)
        i_ref[0, rows, :] = oi_ref[pl.ds(c, PEER_HK, stride=SUBLANES), :].T
        j_ref[0, rows, :] = oj_ref[pl.ds(c, PEER_HK, stride=SUBLANES), :].T
        gate_ref[0, rows, :] = og_ref[pl.ds(c, PEER_HK, stride=SUBLANES), :].T

    tie_ref[...] = tie
    rows_per_chunk = EXACT_CHUNK // LANES

    def chunk(ci, carry):
        flagged = jnp.max(tie_ref[pl.ds(ci * rows_per_chunk, rows_per_chunk), :]) > 0.0

        @pl.when(flagged)
        def _exact():
            r0 = pl.multiple_of(ci * EXACT_CHUNK, EXACT_CHUNK)
            hc = _rms_mod(x_ref[0, pl.ds(r0, EXACT_CHUNK), :], g_ref[...], sh_ref[0], sc_ref[0])
            ii, jj, gg = _route_exact_chunk(hc, wq_ref, kd_ref)
            i_ref[0, pl.ds(r0, EXACT_CHUNK), :] = ii
            j_ref[0, pl.ds(r0, EXACT_CHUNK), :] = jj
            gate_ref[0, pl.ds(r0, EXACT_CHUNK), :] = gg

        return carry

    lax.fori_loop(0, tt // EXACT_CHUNK, chunk, 0)


def _route(x, g, sh, sc, wq, kd):
    b, s, d = x.shape
    tt = ROUTE_TILE
    k = PEER_TOPK
    tok = lambda w: pl.BlockSpec((1, tt, w), lambda i, j: (i, j, 0))
    entry = (SUBLANES, LANES)
    return pl.pallas_call(
        functools.partial(_route_kernel, tt),
        grid=(b, s // tt),
        in_specs=[tok(d), _const_spec((1, d)),
                  pl.BlockSpec((1, 1, d), lambda i, j: (i, 0, 0)),
                  pl.BlockSpec((1, 1, d), lambda i, j: (i, 0, 0)),
                  _const_spec(wq.shape), _const_spec(kd.shape)],
        out_specs=[tok(d), tok(PEER_HK), tok(PEER_HK), tok(PEER_HK)],
        out_shape=[jax.ShapeDtypeStruct((b, s, d), bf16)] + [jax.ShapeDtypeStruct((b, s, PEER_HK), f32)] * 3,
        scratch_shapes=[pltpu.VMEM((tt, d), bf16),
                        pltpu.VMEM((2, 2, (tt // LANES) * STAGE_PITCH, LANES), f32),
                        pltpu.VMEM((PEER_NKEYS // k, k) + entry, f32), pltpu.VMEM((PEER_NKEYS // k, k) + entry, f32),
                        pltpu.VMEM((2, k) + entry, f32), pltpu.VMEM((2, k) + entry, f32),
                        pltpu.VMEM((PEER_HK * SUBLANES, LANES), f32), pltpu.VMEM((PEER_HK * SUBLANES, LANES), f32),
                        pltpu.VMEM((PEER_HK * SUBLANES, LANES), f32), pltpu.VMEM(entry, f32)],
        compiler_params=_params(("parallel", "parallel"), 56),
        name="peer_route",
    )(x, g.reshape(1, d), sh, sc, wq, kd)


EXPERT_BLOCK = 1024
I_PER_BLOCK = EXPERT_BLOCK // PEER_NKEYS
N_EXPERT_BLOCKS = PEER_NKEYS * PEER_NKEYS // EXPERT_BLOCK
G_SLABS = PEER_NKEYS // 2
BLOCKS_PER_HALF = N_EXPERT_BLOCKS // 2
G_PITCH_PAD = SUBLANES
BUILD_UNROLL = 16
ROW_SPLITS = 2
HIGH_HALF = 0xFFFF0000


def _two_gelu(a):
    return a * (1.0 + lax.erf(a * (2.0 ** -0.5)))


def _bf16_bits(a):
    return lax.bitcast_convert_type(a.astype(bf16).astype(f32), jnp.uint32)


def _expert_kernel(tt, final, x_ref, h2_ref, i_ref, j_ref, gate_ref, gt_ref, ut_ref, v_ref, fg_ref, o_ref, g_buf, acc_ref):
    ib = pl.program_id(2)
    pitch = tt + G_PITCH_PAD

    @pl.when(ib == 0)
    def _build():
        sub = lax.broadcasted_iota(jnp.int32, (PEER_NKEYS, PEER_HK), 0).astype(f32).astype(bf16)
        one = jnp.ones((), bf16)
        zero = jnp.zeros((), bf16)

        def body(step, carry):
            for u in range(BUILD_UNROLL):
                t = step * BUILD_UNROLL + u
                irow = i_ref[0, pl.ds(t, 1), :].astype(bf16)
                jrow = j_ref[0, pl.ds(t, 1), :].astype(bf16)
                grow = (0.5 * gate_ref[0, pl.ds(t, 1), :]).astype(bf16)
                pt = jnp.where(sub == irow, one, zero)
                qt = jnp.where(sub == jrow, grow, zero)
                gt = _dot_nt(pt, qt)
                packed = lax.bitwise_or(_bf16_bits(gt[G_SLABS:]),
                                        lax.shift_right_logical(_bf16_bits(gt[:G_SLABS]), jnp.uint32(16)))
                g_buf[pl.ds(t, G_SLABS, stride=pitch), :] = packed
            return carry

        lax.fori_loop(0, tt // BUILD_UNROLL, body, 0)
        acc_ref[...] = jnp.zeros_like(acc_ref)

    slab0 = lax.rem(ib, BLOCKS_PER_HALF) * I_PER_BLOCK
    shift = jnp.where(ib < BLOCKS_PER_HALF, 16, 0).astype(jnp.uint32)
    rows = tt // ROW_SPLITS
    for r in range(0, tt, rows):
        a = _dot(h2_ref[0, r:r + rows, :], ut_ref[...])
        pieces = []
        for k in range(I_PER_BLOCK):
            row0 = pl.multiple_of((slab0 + k) * pitch + r, SUBLANES)
            bits = lax.bitwise_and(lax.shift_left(g_buf[pl.ds(row0, rows), :], shift), jnp.uint32(HIGH_HALF))
            gk = lax.bitcast_convert_type(bits, f32)
            pieces.append((gk * _two_gelu(a[:, k * PEER_NKEYS:(k + 1) * PEER_NKEYS])).astype(bf16))
        acc_ref[r:r + rows, :] += _dot(jnp.concatenate(pieces, axis=1), v_ref[...])

    @pl.when(ib == N_EXPERT_BLOCKS - 1)
    def _finish():
        y = x_ref[0] + gt_ref[0] * acc_ref[...]
        if final:
            ms = jnp.mean(y * y, axis=-1, keepdims=True)
            y = y * lax.rsqrt(ms + NORM_EPS) * fg_ref[...]
        o_ref[0] = y


def _experts(x, h2, idx_i, idx_j, gate, gt, ut, v, final_g, final, tt):
    b, s, d = x.shape
    tok = lambda w: pl.BlockSpec((1, tt, w), lambda i, j, e: (i, j, 0))
    return pl.pallas_call(
        functools.partial(_expert_kernel, tt, final),
        grid=(b, s // tt, N_EXPERT_BLOCKS),
        in_specs=[tok(d), tok(d), tok(PEER_HK), tok(PEER_HK), tok(PEER_HK),
                  pl.BlockSpec((1, 1, d), lambda i, j, e: (i, 0, 0)),
                  pl.BlockSpec((d, EXPERT_BLOCK), lambda i, j, e: (0, e)),
                  pl.BlockSpec((EXPERT_BLOCK, d), lambda i, j, e: (e, 0)),
                  pl.BlockSpec((1, d), lambda i, j, e: (0, 0))],
        out_specs=tok(d),
        out_shape=jax.ShapeDtypeStruct((b, s, d), f32),
        scratch_shapes=[pltpu.VMEM((G_SLABS * (tt + G_PITCH_PAD), PEER_NKEYS), jnp.uint32),
                        pltpu.VMEM((tt, d), f32)],
        compiler_params=_params(("parallel", "parallel", "arbitrary"), 56),
        name="peer_experts",
    )(x, h2, idx_i, idx_j, gate, gt, ut, v, final_g.reshape(1, d))


def _rope_tables(seq):
    t = jnp.arange(seq)
    pos = jnp.stack([t // GRID_W, t % GRID_W], axis=1).astype(f32)
    f = HEAD_DIM // 4
    inv = ROPE_BASE ** (-jnp.arange(f, dtype=f32) / f)
    lane = jnp.arange(LANES)
    l64 = lane % HEAD_DIM
    ang = pos[:, (l64 // (2 * f))] * inv[l64 % f][None, :]
    first = (l64 % (2 * f)) < f
    cos = jnp.cos(ang)
    sin = jnp.sin(ang)
    return cos, jnp.where(first[None, :], -sin, 0.0), jnp.where(first[None, :], 0.0, sin)


def _na_bias_table(rpb, rows):
    r0 = np.array(_na_class_rows(rows))[:, None, None]
    a = np.arange(NA_ROWS)[None, :, None]
    kr = np.arange(NA_SPAN)[None, None, :]
    r = r0 + a
    ws = np.clip(r - NA_KH // 2, 0, rows - NA_KH)
    absrow = np.clip(r0 - NA_KH // 2, 0, rows - NA_SPAN) + kr
    ok_r = (absrow >= ws) & (absrow < ws + NA_KH)
    dr = absrow - r + NA_KH - 1
    oh_r = (ok_r[..., None] & (dr[..., None] == np.arange(2 * NA_KH - 1))).astype(np.float32)
    qc = np.arange(GRID_W)[:, None]
    kc = np.arange(GRID_W)[None, :]
    cstart = np.clip(qc - NA_KW // 2, 0, GRID_W - NA_KW)
    ok_c = (kc >= cstart) & (kc < cstart + NA_KW)
    dc = kc - qc + NA_KW - 1
    oh_c = (ok_c[..., None] & (dc[..., None] == np.arange(2 * NA_KW - 1))).astype(np.float32)
    hp = lax.Precision.HIGHEST
    t1 = jnp.einsum("hde,qke->hdqk", rpb.astype(f32), jnp.asarray(oh_c), precision=hp)
    bias = jnp.einsum("cafd,hdqk->hcaqfk", jnp.asarray(oh_r), t1, precision=hp)
    ok = ok_r[None, :, :, None, :, None] & ok_c[None, None, None, :, None, :]
    bias = jnp.where(jnp.asarray(ok), bias, NEG_BIG)
    return bias.reshape(NA_HEADS, NA_CLASSES, NA_ROWS * GRID_W, NA_SPAN * GRID_W)


def _layout_w_in(w):
    c = lambda off, width: w[:, off:off + width]
    hd = HEAD_DIM
    k0, k1 = c(OFF_KW, hd), c(OFF_KW + hd, hd)
    v0, v1 = c(OFF_VW, hd), c(OFF_VW + hd, hd)
    cols = [c(OFF_KN, NA_WIDTH), c(OFF_VN, NA_WIDTH), k0, k0, k1, k1, v0, v0, v1, v1,
            c(OFF_QN, NA_WIDTH), c(OFF_QW, SWA_Q_WIDTH), c(OFF_A, POOL_WIDTH), c(OFF_G, 3 * D_MODEL)]
    return jnp.concatenate(cols, axis=1).astype(bf16)


def _layout_peer_queries(wq):
    d = wq.shape[0]
    return wq.reshape(d, PEER_HEADS, 2 * PEER_NKEYS).transpose(1, 0, 2)


def _layout_peer_keys(keys):
    h, _, nk, dk = keys.shape
    z = jnp.zeros((h, nk, dk), keys.dtype)
    return jnp.concatenate([jnp.concatenate([keys[:, 0], z], axis=2), jnp.concatenate([z, keys[:, 1]], axis=2)], axis=1)


LATENT_TILE = 512
EXPERT_TILE = 512


def kernel(x, c, ctx, c_ctx, ada_w, ada_b, norm1_g, norm2_g, w_in, pool_w, pool_scale, na_rpb, swa_sink,
           w_branch_a, w_branch_b, w_branch_c, w_out, peer_wq, peer_keys, peer_u, peer_v, final_g):
    b, s, d = x.shape
    lc = ctx.shape[1]
    depth = ada_w.shape[0]
    cpad = jnp.concatenate([c, c_ctx[None, :], jnp.zeros((SUBLANES - b - 1, d), f32)], axis=0)
    mod = _modulation(cpad, ada_w, ada_b)
    rope_tabs = _rope_tables(s)
    xc = ctx
    for l in range(depth):
        last = l == depth - 1
        ml = mod[l, :b].reshape(b, 6, 1, d)
        sh1, sc1, gt1, sh2, sc2, gt2 = (ml[:, i] for i in range(6))
        mc = jnp.broadcast_to(mod[l, b].reshape(1, 6, 1, d), (b, 6, 1, d))
        csh1, csc1, cgt1, csh2, csc2, cgt2 = (mc[:, i] for i in range(6))

        w_aug = _layout_w_in(w_in[l])
        wa, wb, wc, wo = (w[l].astype(bf16) for w in (w_branch_a, w_branch_b, w_branch_c, w_out))
        pw = pool_w[l].astype(bf16)
        bias_tbl = _na_bias_table(na_rpb[l], s // GRID_W)
        wq = _layout_peer_queries(peer_wq[l]).astype(bf16)
        kd = _layout_peer_keys(peer_keys[l]).astype(bf16)
        ut = peer_u[l].T.astype(bf16)
        vb = peer_v[l].astype(bf16)

        knvn, sw, q, a, g = _inproj(x, norm1_g[l], sh1, sc1, w_aug, SEGS_ALL, rope_tabs, LATENT_TILE)
        if last:
            knvn_c, sw_c = _inproj(xc, norm1_g[l], csh1, csc1, w_aug, SEGS_KV, None, lc)
        else:
            knvn_c, sw_c, q_c, a_c, g_c = _inproj(xc, norm1_g[l], csh1, csc1, w_aug, SEGS_ALL, None, lc)
        ya = _pool(a, pw, pool_scale[l], 256)
        ob = _na(q, knvn, knvn_c, bias_tbl)
        oc = _swa(q, sw, sw_c, swa_sink[l])
        x = _merge(x, ya, ob, oc, g, gt1, wa, wb, wc, wo, LATENT_TILE)
        if not last:
            ya_c = _pool(a_c, pw, pool_scale[l], lc)
            ob_c, oc_c = _ctx_attn(q_c, knvn_c, sw_c, swa_sink[l])
            xc = _merge(xc, ya_c, ob_c, oc_c, g_c, cgt1, wa, wb, wc, wo, lc)

        h2, ii, jj, gate = _route(x, norm2_g[l], sh2, sc2, wq, kd)
        x = _experts(x, h2, ii, jj, gate, gt2, ut, vb, final_g, last, EXPERT_TILE)
        if not last:
            xf = xc.reshape(1, b * lc, d)
            h2c, iic, jjc, gatec = _route(xf, norm2_g[l], csh2[:1], csc2[:1], wq, kd)
            xc = _experts(xf, h2c, iic, jjc, gatec, cgt2[:1], ut, vb, final_g, False, EXPERT_TILE).reshape(b, lc, d)
    return x
```

```python
import functools

import jax
import jax.numpy as jnp
import numpy as np
from jax import lax
from jax.experimental import pallas as pl
from jax.experimental.pallas import tpu as pltpu

f32 = jnp.float32
bf16 = jnp.bfloat16

D_MODEL = 1024
GRID_W = 64
HEAD_DIM = 64
NORM_EPS = 1e-6
ROPE_BASE = 10000.0

POOL_WIDTH = 512
POOL_WINDOWS = (2, 4, 8, 16)
POOL_GROUP_W = 128
POOL_HALO = 8

NA_HEADS = 8
NA_KH = 8
NA_KW = 16

SWA_HEADS = 8
SWA_KV_HEADS = 2
SWA_WINDOW = 128
SWA_BLOCK = 128

PEER_HEADS = 8
PEER_NKEYS = 128
PEER_TOPK = 16
PEER_HK = PEER_HEADS * PEER_TOPK

NA_WIDTH = NA_HEADS * HEAD_DIM
SWA_Q_WIDTH = SWA_HEADS * HEAD_DIM
SWA_KV_WIDTH = SWA_KV_HEADS * HEAD_DIM
OFF_KN = 0
OFF_VN = OFF_KN + NA_WIDTH
OFF_KW = OFF_VN + NA_WIDTH
OFF_VW = OFF_KW + SWA_KV_WIDTH
OFF_QN = OFF_VW + SWA_KV_WIDTH
OFF_QW = OFF_QN + NA_WIDTH
OFF_A = OFF_QW + SWA_Q_WIDTH
OFF_G = OFF_A + POOL_WIDTH

LANES = 128
SUBLANES = 8
NEG_BIG = -1e30
MIB = 1024 * 1024

SEG_KN = ("knvn", 0, 1024)
SEG_SW = ("sw", 1024, 512)
SEG_Q = ("q", 1536, 1024)
SEG_A = ("a", 2560, 512)
SEG_G = ("g", 3072, 3072)
SEGS_ALL = (SEG_KN, SEG_SW, SEG_Q, SEG_A, SEG_G)
SEGS_KV = (SEG_KN, SEG_SW)
SEG_DTYPE = {"knvn": bf16, "sw": bf16, "q": bf16, "a": f32, "g": bf16}
COL_CHUNK = 512


def _params(sem, vmem_mib):
    return pltpu.CompilerParams(dimension_semantics=sem, vmem_limit_bytes=vmem_mib * MIB)


def _const_spec(shape):
    nd = len(shape)
    return pl.BlockSpec(shape, lambda *_: (0,) * nd, pipeline_mode=pl.Buffered(1))


def _dot(a, b):
    return jnp.dot(a, b, preferred_element_type=f32)


def _dot_nt(a, b):
    return lax.dot_general(a, b, (((1,), (1,)), ((), ())), preferred_element_type=f32)


def _split(a):
    hi = a.astype(bf16)
    lo = (a - hi.astype(f32)).astype(bf16)
    return hi, lo


def _sigmoid(z):
    return 1.0 / (1.0 + jnp.exp(-z))


def _rms_mod(x, g, sh, sc):
    ms = jnp.mean(x * x, axis=-1, keepdims=True)
    h = x * lax.rsqrt(ms + NORM_EPS) * g
    return h * (1.0 + sc) + sh


def _mod_kernel(c_ref, w_ref, b_ref, o_ref):
    c = c_ref[...]
    s = c * _sigmoid(c)
    sh, sl = _split(s)
    wh, wl = _split(w_ref[0])
    o_ref[0] = _dot(sh, wh) + _dot(sh, wl) + _dot(sl, wh) + b_ref[0]


def _modulation(cpad, ada_w, ada_b):
    depth, d, n = ada_w.shape
    tn = 1536
    return pl.pallas_call(
        _mod_kernel,
        grid=(depth, n // tn),
        in_specs=[
            pl.BlockSpec((SUBLANES, d), lambda l, j: (0, 0)),
            pl.BlockSpec((1, d, tn), lambda l, j: (l, 0, j)),
            pl.BlockSpec((1, 1, tn), lambda l, j: (l, 0, j)),
        ],
        out_specs=pl.BlockSpec((1, SUBLANES, tn), lambda l, j: (l, 0, j)),
        out_shape=jax.ShapeDtypeStruct((depth, SUBLANES, n), f32),
        compiler_params=_params(("arbitrary", "arbitrary"), 40),
        name="adaln_mod",
    )(cpad, ada_w, ada_b.reshape(depth, 1, n))


def _rope(z, cos, s1, s2):
    return z * cos + pltpu.roll(z, LANES - 16, 1) * s1 + pltpu.roll(z, 16, 1) * s2


def _inproj_kernel(segs, rope, x_ref, g_ref, sh_ref, sc_ref, w_ref, *rest):
    if rope:
        cos_ref, s1_ref, s2_ref = rest[:3]
        outs = rest[3:]
        cos, s1, s2 = cos_ref[...], s1_ref[...], s2_ref[...]
    else:
        outs = rest
    hb = _rms_mod(x_ref[0], g_ref[...], sh_ref[0], sc_ref[0]).astype(bf16)
    for (name, c0, width), o_ref in zip(segs, outs):
        for j in range(0, width, COL_CHUNK):
            z = _dot(hb, w_ref[:, c0 + j:c0 + j + COL_CHUNK])
            if name == "g":
                z = _sigmoid(z)
            if rope and name in ("sw", "q"):
                n_rot = 2 if name == "sw" else (4 if j == COL_CHUNK else 0)
                slabs = [z[:, s * LANES:(s + 1) * LANES] for s in range(COL_CHUNK // LANES)]
                slabs = [_rope(sl, cos, s1, s2) if s < n_rot else sl for s, sl in enumerate(slabs)]
                z = jnp.concatenate(slabs, axis=1)
            o_ref[0, :, j:j + COL_CHUNK] = z.astype(o_ref.dtype)


def _inproj(x, g, sh, sc, w_aug, segs, rope_tabs, tm):
    b, s, d = x.shape
    rope = rope_tabs is not None
    in_specs = [
        pl.BlockSpec((1, tm, d), lambda i, j: (i, j, 0)),
        _const_spec((1, d)),
        pl.BlockSpec((1, 1, d), lambda i, j: (i, 0, 0)),
        pl.BlockSpec((1, 1, d), lambda i, j: (i, 0, 0)),
        _const_spec(w_aug.shape),
    ]
    args = [x, g.reshape(1, d), sh, sc, w_aug]
    if rope:
        in_specs += [pl.BlockSpec((tm, LANES), lambda i, j: (j, 0))] * 3
        args += list(rope_tabs)
    out_specs = [pl.BlockSpec((1, tm, w), lambda i, j: (i, j, 0)) for _, _, w in segs]
    out_shape = [jax.ShapeDtypeStruct((b, s, w), SEG_DTYPE[n]) for n, _, w in segs]
    return pl.pallas_call(
        functools.partial(_inproj_kernel, segs, rope),
        grid=(b, s // tm),
        in_specs=in_specs,
        out_specs=out_specs,
        out_shape=out_shape,
        compiler_params=_params(("parallel", "parallel"), 52),
        name="inproj",
    )(*args)


def _pool_kernel(seq, tt, a_ref, pw_ref, ps_ref, o_ref):
    t0 = pl.multiple_of(pl.program_id(1) * tt, tt)
    cur = a_ref[0, pl.ds(t0, tt), :]
    p0 = pl.multiple_of(jnp.maximum(t0 - POOL_HALO, 0), POOL_HALO)
    n0 = pl.multiple_of(jnp.minimum(t0 + tt, seq - POOL_HALO), POOL_HALO)
    prev = jnp.where(t0 > 0, a_ref[0, pl.ds(p0, POOL_HALO), :], 0.0)
    nxt = jnp.where(t0 + tt < seq, a_ref[0, pl.ds(n0, POOL_HALO), :], 0.0)
    ext = jnp.concatenate([prev, cur, nxt], axis=0)
    n_ext = tt + 2 * POOL_HALO
    tpos = t0 + lax.broadcasted_iota(jnp.int32, (tt, 1), 0)
    for g, w in enumerate(POOL_WINDOWS):
        sl = slice(g * POOL_GROUP_W, (g + 1) * POOL_GROUP_W)
        eg = ext[:, sl]
        acc = jnp.zeros((tt, POOL_GROUP_W), f32)
        for k in range(-(w // 2), w - w // 2):
            shifted = eg if k == 0 else pltpu.roll(eg, (-k) % n_ext, 0)
            acc = acc + shifted[POOL_HALO:POOL_HALO + tt]
        lo = jnp.clip(tpos - w // 2, 0, seq)
        hi = jnp.clip(tpos - w // 2 + w, 0, seq)
        pooled = acc / (hi - lo).astype(f32) - cur[:, sl]
        y = _dot(pooled.astype(bf16), pw_ref[g]) * ps_ref[:, sl]
        o_ref[0, :, sl] = y.astype(o_ref.dtype)


def _pool(a, pool_w, pool_scale, tt):
    b, s, w = a.shape
    return pl.pallas_call(
        functools.partial(_pool_kernel, s, tt),
        grid=(b, s // tt),
        in_specs=[
            pl.BlockSpec((1, s, w), lambda i, j: (i, 0, 0)),
            _const_spec(pool_w.shape),
            _const_spec((1, w)),
        ],
        out_specs=pl.BlockSpec((1, tt, w), lambda i, j: (i, j, 0)),
        out_shape=jax.ShapeDtypeStruct((b, s, w), bf16),
        compiler_params=_params(("parallel", "arbitrary"), 40),
        name="pool_mixer",
    )(a, pool_w, pool_scale.reshape(1, w))


def _half_masks():
    lane = lax.broadcasted_iota(jnp.int32, (1, LANES), 1)
    lo = lane < HEAD_DIM
    return lo, jnp.logical_not(lo)


def _attend_separate(logits, vals, extra_logit):
    m = functools.reduce(jnp.maximum, [jnp.max(s, axis=-1, keepdims=True) for s in logits])
    if extra_logit is not None:
        m = jnp.maximum(m, extra_logit)
    den = None
    out = None
    for s, v in zip(logits, vals):
        e = jnp.exp(s - m)
        d = jnp.sum(e, axis=-1, keepdims=True)
        o = _dot(e.astype(bf16), v)
        den = d if den is None else den + d
        out = o if out is None else out + o
    if extra_logit is not None:
        den = den + jnp.exp(extra_logit - m)
    return out / den


def _attend(qm, keys, vals, biases, extra_logit):
    scale = HEAD_DIM ** -0.5
    logits = []
    for k, bias in zip(keys, biases):
        s = _dot_nt(qm, k) * scale
        if bias is not None:
            s = s + bias
        logits.append(s)
    if any(k.shape[0] % LANES for k in keys):
        return _attend_separate(logits, vals, extra_logit)
    s = jnp.concatenate(logits, axis=-1)
    m = jnp.max(s, axis=-1, keepdims=True)
    if extra_logit is not None:
        m = jnp.maximum(m, extra_logit)
    e = jnp.exp(s - m)
    den = jnp.sum(e, axis=-1, keepdims=True)
    if extra_logit is not None:
        den = den + jnp.exp(extra_logit - m)
    eb = e.astype(bf16)
    out = None
    c0 = 0
    for k, v in zip(keys, vals):
        o = _dot(eb[:, c0:c0 + k.shape[0]], v)
        out = o if out is None else out + o
        c0 += k.shape[0]
    return out / den


NA_ROWS = 2
NA_SPAN = NA_KH + NA_ROWS - 1
NA_CLASSES = 5


def _na_class_rows(rows):
    return (0, 2, 4, rows - 4, rows - 2)


def _na_kernel(rows, q_ref, k_ref, v_ref, kc_ref, vc_ref, bias_ref, o_ref):
    r0 = pl.program_id(1) * NA_ROWS
    k0 = pl.multiple_of(jnp.clip(r0 - NA_KH // 2, 0, rows - NA_SPAN) * GRID_W, GRID_W)
    nk = NA_SPAN * GRID_W
    nq = NA_ROWS * GRID_W
    lo, hi = _half_masks()
    for p in range(NA_HEADS // 2):
        sl = slice(p * LANES, (p + 1) * LANES)
        qp = q_ref[0, :, sl]
        zero = jnp.zeros_like(qp)
        qs = jnp.concatenate([jnp.where(lo, qp, zero), jnp.where(hi, qp, zero)], axis=0)
        bias = jnp.concatenate([bias_ref[2 * p, 0], bias_ref[2 * p + 1, 0]], axis=0)
        o = _attend(qs, (kc_ref[0, :, sl], k_ref[0, pl.ds(k0, nk), sl]),
                    (vc_ref[0, :, sl], v_ref[0, pl.ds(k0, nk), sl]), (None, bias), None)
        o_ref[0, :, sl] = jnp.where(lo, o[:nq], o[nq:]).astype(o_ref.dtype)


def _na(q, knvn, knvn_c, bias_tbl):
    b, s, _ = q.shape
    rows = s // GRID_W
    lc = knvn_c.shape[1]
    nq = NA_ROWS * GRID_W

    def cls(i):
        r0 = i * NA_ROWS
        return jnp.where(r0 < 4, r0 // 2, jnp.where(r0 >= rows - 4, 3 + (r0 - (rows - 4)) // 2, 2))

    return pl.pallas_call(
        functools.partial(_na_kernel, rows),
        grid=(b, rows // NA_ROWS),
        in_specs=[
            pl.BlockSpec((1, nq, NA_WIDTH), lambda i, r: (i, r, 0)),
            pl.BlockSpec((1, s, NA_WIDTH), lambda i, r: (i, 0, 0)),
            pl.BlockSpec((1, s, NA_WIDTH), lambda i, r: (i, 0, 1)),
            pl.BlockSpec((1, lc, NA_WIDTH), lambda i, r: (i, 0, 0)),
            pl.BlockSpec((1, lc, NA_WIDTH), lambda i, r: (i, 0, 1)),
            pl.BlockSpec((NA_HEADS, 1, nq, NA_SPAN * GRID_W), lambda i, r: (0, cls(r), 0, 0)),
        ],
        out_specs=pl.BlockSpec((1, nq, NA_WIDTH), lambda i, r: (i, r, 0)),
        out_shape=jax.ShapeDtypeStruct((b, s, NA_WIDTH), bf16),
        compiler_params=_params(("parallel", "arbitrary"), 48),
        name="nbr_attn",
    )(q, knvn, knvn, knvn_c, knvn_c, bias_tbl)


def _swa_kernel(seq, q_ref, sw_ref, swc_ref, sink_ref, o_ref):
    n = pl.program_id(1)
    span = 3 * SWA_BLOCK
    start = pl.multiple_of(jnp.clip((n - 1) * SWA_BLOCK, 0, seq - span), SWA_BLOCK)
    qpos = n * SWA_BLOCK + lax.broadcasted_iota(jnp.int32, (SWA_BLOCK, 1), 0)
    kpos = start + lax.broadcasted_iota(jnp.int32, (1, span), 1)
    bias1 = jnp.where(jnp.abs(qpos - kpos) <= SWA_WINDOW, 0.0, NEG_BIG)
    lo, hi = _half_masks()
    rep = SWA_HEADS // SWA_KV_HEADS
    bias = jnp.concatenate([bias1] * rep, axis=0)
    for g in range(SWA_KV_HEADS):
        kd = sw_ref[0, pl.ds(start, span), g * LANES:(g + 1) * LANES]
        vd = sw_ref[0, pl.ds(start, span), (2 + g) * LANES:(3 + g) * LANES]
        kcd = swc_ref[0, :, g * LANES:(g + 1) * LANES]
        vcd = swc_ref[0, :, (2 + g) * LANES:(3 + g) * LANES]
        qs, sinks = [], []
        for h in range(g * rep, (g + 1) * rep):
            qp = q_ref[0, :, (h // 2) * LANES:(h // 2 + 1) * LANES]
            qs.append(jnp.where(hi if h % 2 else lo, qp, jnp.zeros_like(qp)))
            sinks.append(jnp.full((SWA_BLOCK, 1), sink_ref[h], f32))
        o = _attend(jnp.concatenate(qs, axis=0), (kd, kcd), (vd, vcd), (bias, None), jnp.concatenate(sinks, axis=0))
        for j in range(rep // 2):
            p = (g * rep) // 2 + j
            o_ref[0, :, p * LANES:(p + 1) * LANES] = jnp.where(
                lo, o[2 * j * SWA_BLOCK:(2 * j + 1) * SWA_BLOCK], o[(2 * j + 1) * SWA_BLOCK:(2 * j + 2) * SWA_BLOCK]
            ).astype(o_ref.dtype)


def _swa(q, sw, sw_c, sink):
    b, s, _ = q.shape
    lc = sw_c.shape[1]
    return pl.pallas_call(
        functools.partial(_swa_kernel, s),
        grid=(b, s // SWA_BLOCK),
        in_specs=[
            pl.BlockSpec((1, SWA_BLOCK, SWA_Q_WIDTH), lambda i, n: (i, n, 1)),
            pl.BlockSpec((1, s, 512), lambda i, n: (i, 0, 0)),
            pl.BlockSpec((1, lc, 512), lambda i, n: (i, 0, 0)),
            pl.BlockSpec(memory_space=pltpu.SMEM),
        ],
        out_specs=pl.BlockSpec((1, SWA_BLOCK, SWA_Q_WIDTH), lambda i, n: (i, n, 0)),
        out_shape=jax.ShapeDtypeStruct((b, s, SWA_Q_WIDTH), bf16),
        compiler_params=_params(("parallel", "arbitrary"), 40),
        name="win_attn",
    )(q, sw, sw_c, sink)


def _ctx_attn_kernel(q_ref, kv_ref, sw_ref, sink_ref, ob_ref, oc_ref):
    lo, hi = _half_masks()
    rep = SWA_HEADS // SWA_KV_HEADS
    for p in range(NA_HEADS // 2):
        sl = slice(p * LANES, (p + 1) * LANES)
        qp = q_ref[0, :, sl]
        kp = kv_ref[0, :, sl]
        vp = kv_ref[0, :, NA_WIDTH + p * LANES:NA_WIDTH + (p + 1) * LANES]
        halves = [_attend(jnp.where(msk, qp, jnp.zeros_like(qp)), (kp,), (vp,), (None,), None) for msk in (lo, hi)]
        ob_ref[0, :, sl] = jnp.where(lo, halves[0], halves[1]).astype(ob_ref.dtype)
    outs = []
    for h in range(SWA_HEADS):
        g = h // rep
        kd = sw_ref[0, :, g * LANES:(g + 1) * LANES]
        vd = sw_ref[0, :, (2 + g) * LANES:(3 + g) * LANES]
        qp = q_ref[0, :, NA_WIDTH + (h // 2) * LANES:NA_WIDTH + (h // 2 + 1) * LANES]
        qm = jnp.where(hi if h % 2 else lo, qp, jnp.zeros_like(qp))
        outs.append(_attend(qm, (kd,), (vd,), (None,), sink_ref[h]))
    for p in range(SWA_HEADS // 2):
        oc_ref[0, :, p * LANES:(p + 1) * LANES] = jnp.where(lo, outs[2 * p], outs[2 * p + 1]).astype(oc_ref.dtype)


def _ctx_attn(q_c, knvn_c, sw_c, sink):
    b, lc, _ = q_c.shape
    spec = lambda w: pl.BlockSpec((1, lc, w), lambda i: (i, 0, 0))
    return pl.pallas_call(
        _ctx_attn_kernel,
        grid=(b,),
        in_specs=[spec(1024), spec(1024), spec(512), pl.BlockSpec(memory_space=pltpu.SMEM)],
        out_specs=[spec(NA_WIDTH), spec(SWA_Q_WIDTH)],
        out_shape=[jax.ShapeDtypeStruct((b, lc, NA_WIDTH), bf16), jax.ShapeDtypeStruct((b, lc, SWA_Q_WIDTH), bf16)],
        compiler_params=_params(("parallel",), 40),
        name="ctx_attn",
    )(q_c, knvn_c, sw_c, sink)


def _merge_kernel(x_ref, ya_ref, ob_ref, oc_ref, g_ref, gt_ref, wa_ref, wb_ref, wc_ref, wo_ref, o_ref):
    d = D_MODEL
    m = g_ref[0, :, 0:d].astype(f32) * _dot(ya_ref[0], wa_ref[...])
    m = m + g_ref[0, :, d:2 * d].astype(f32) * _dot(ob_ref[0], wb_ref[...])
    m = m + g_ref[0, :, 2 * d:3 * d].astype(f32) * _dot(oc_ref[0], wc_ref[...])
    o_ref[0] = x_ref[0] + gt_ref[0] * _dot(m.astype(bf16), wo_ref[...])


def _merge(x, ya, ob, oc, g, gt, wa, wb, wc, wo, tm):
    b, s, d = x.shape
    tok = lambda w: pl.BlockSpec((1, tm, w), lambda i, j: (i, j, 0))
    return pl.pallas_call(
        _merge_kernel,
        grid=(b, s // tm),
        in_specs=[tok(d), tok(512), tok(512), tok(512), tok(3 * d),
                  pl.BlockSpec((1, 1, d), lambda i, j: (i, 0, 0)),
                  _const_spec(wa.shape), _const_spec(wb.shape), _const_spec(wc.shape), _const_spec(wo.shape)],
        out_specs=tok(d),
        out_shape=jax.ShapeDtypeStruct((b, s, d), f32),
        compiler_params=_params(("parallel", "parallel"), 48),
        name="merge_out",
    )(x, ya, ob, oc, g, gt, wa, wb, wc, wo)


def _topk_rows(s, k):
    n = s.shape[0]
    rid = lax.broadcasted_iota(jnp.int32, s.shape, 0)
    vals, idxs = [], []
    for _ in range(k):
        m = jnp.max(s, axis=0, keepdims=True)
        i = jnp.min(jnp.where(s == m, rid, n), axis=0, keepdims=True)
        vals.append(m)
        idxs.append(i)
        s = jnp.where(rid == i, -jnp.inf, s)
    return jnp.concatenate(vals, axis=0), jnp.concatenate(idxs, axis=0)


def _pick_rows(tab, sel):
    rid = lax.broadcasted_iota(jnp.int32, tab.shape, 0)
    rows = []
    for k in range(sel.shape[0]):
        rows.append(jnp.sum(jnp.where(rid == sel[k:k + 1, :], tab, 0.0), axis=0, keepdims=True))
    return jnp.concatenate(rows, axis=0)


def _route_exact_chunk(h2, wq_ref, kd_ref):
    hb = h2.astype(bf16)
    k = PEER_TOPK
    nk = PEER_NKEYS
    i_rows, j_rows, g_rows = [], [], []
    for h in range(PEER_HEADS):
        top_s, top_i = [], []
        for p in range(2):
            cs = slice(p * nk, (p + 1) * nk)
            q = _dot(hb, wq_ref[h, :, cs])
            st = _dot_nt(kd_ref[h, cs, cs], q.astype(bf16))
            ts, ti = _topk_rows(st, k)
            top_s.append(ts)
            top_i.append(ti.astype(f32))
        blocks = [top_s[0][0:1] + top_s[1][0:8], top_s[0][0:1] + top_s[1][8:16]]
        blocks += [top_s[0][a:a + 1] + top_s[1][0:8] for a in range(1, 8)]
        blocks += [top_s[0][8:16] + top_s[1][0:1]]
        best_s, row = _topk_rows(jnp.concatenate(blocks, axis=0), k)
        blk = lax.shift_right_logical(row, 3)
        sub = lax.bitwise_and(row, 7)
        tail = blk == len(blocks) - 1
        sel_a = jnp.where(tail, sub + 8, jnp.maximum(blk - 1, 0))
        sel_b = jnp.where(tail, 0, jnp.where(blk == 1, sub + 8, sub))
        i_rows.append(_pick_rows(top_i[0], sel_a))
        j_rows.append(_pick_rows(top_i[1], sel_b))
        e = jnp.exp(best_s - best_s[0:1, :])
        g_rows.append(e / jnp.sum(e, axis=0, keepdims=True))
    return (jnp.concatenate(i_rows, axis=0).T, jnp.concatenate(j_rows, axis=0).T, jnp.concatenate(g_rows, axis=0).T)


ROUTE_TILE = SUBLANES * LANES
STAGE_PITCH = PEER_NKEYS + SUBLANES
EXACT_CHUNK = 256


def _batcher_pairs(n):
    pairs, p = [], 1
    while p < n:
        k = p
        while k >= 1:
            for j in range(k % p, n - k, 2 * k):
                for i in range(min(k, n - j - k)):
                    if (i + j) // (2 * p) == (i + j + k) // (2 * p):
                        pairs.append((i + j, i + j + k))
            k //= 2
        p *= 2
    return tuple(pairs)


def _bitonic_pairs(n):
    pairs, j = [], n // 2
    while j >= 1:
        pairs += [(i, i + j) for i in range(n) if (i & j) == 0]
        j //= 2
    return tuple(pairs)


def _swap_by_value(x, y):
    return y[0] > x[0]


def _swap_by_value_then_index(x, y):
    return jnp.logical_or(y[0] > x[0], jnp.logical_and(y[0] == x[0], y[1] < x[1]))


def _cmpx(x, y, swap_fn, want_lo=True):
    swap = swap_fn(x, y)
    hi = tuple(jnp.where(swap, b, a) for a, b in zip(x, y))
    lo = tuple(jnp.where(swap, a, b) for a, b in zip(x, y)) if want_lo else None
    return hi, lo


def _run_net(items, pairs, swap_fn):
    items = list(items)
    for i, j in pairs:
        a, b = items[i], items[j]
        if b is None:
            continue
        if a is None:
            items[i], items[j] = b, None
        else:
            items[i], items[j] = _cmpx(a, b, swap_fn)
    return items


def _merge_all(a, b, swap_fn):
    return _run_net(list(a) + list(b)[::-1], _bitonic_pairs(2 * len(a)), swap_fn)


def _merge_top(a, b, swap_fn, dropped=None):
    m = len(a)
    top = []
    for k in range(m):
        x, y = a[k], b[m - 1 - k]
        if x is None or y is None:
            top.append(y if x is None else x)
        else:
            hi, lo = _cmpx(x, y, swap_fn, want_lo=dropped is not None)
            top.append(hi)
            if dropped is not None:
                dropped[0] = jnp.maximum(dropped[0], lo[0])
    return _run_net(top, _bitonic_pairs(m), swap_fn)


def _top_pairs(pair, neg):
    def row(a, n, width):
        return [pair(a, b) for b in range(n)] + [None] * (width - n)

    dropped = [neg]
    by_value = _swap_by_value
    x1 = _merge_all(row(1, 8, 8), [pair(a, 0) for a in range(8, 16)], by_value)
    x2 = _merge_all(row(2, 5, 8), row(3, 4, 8), by_value)
    x3 = _merge_all(row(4, 3, 4), row(5, 2, 4), by_value)
    x4 = _merge_all(row(6, 2, 2), row(7, 2, 2), by_value)
    x5 = _merge_all(x3, x4 + [None] * 4, by_value)
    y1 = _merge_top(row(0, 16, 16), x1, by_value, dropped)
    y2 = _merge_top(x2, x5, by_value, dropped)
    return _merge_top(y1, y2, by_value, dropped), dropped[0]


def _route_kernel(tt, x_ref, g_ref, sh_ref, sc_ref, wq_ref, kd_ref,
                  h2_ref, i_ref, j_ref, gate_ref,
                  hb_ref, st_ref, lv_ref, li_ref, ts_ref, ti_ref, oi_ref, oj_ref, og_ref, tie_ref):
    k = PEER_TOPK
    nk = PEER_NKEYS
    hb = _rms_mod(x_ref[0], g_ref[...], sh_ref[0], sc_ref[0]).astype(bf16)
    h2_ref[0] = hb
    hb_ref[...] = hb
    neg = jnp.full((SUBLANES, LANES), -jnp.inf, f32)

    def scores(h, slot):
        q = _dot(hb_ref[...], wq_ref[h])
        st = _dot_nt(kd_ref[h], q.astype(bf16))
        for p in range(2):
            for c in range(tt // LANES):
                st_ref[slot, p, c * STAGE_PITCH:c * STAGE_PITCH + nk, :] = (
                    st[p * nk:(p + 1) * nk, c * LANES:(c + 1) * LANES])

    scores(0, 0)

    def head(h, tie):
        slot = lax.rem(h, 2)
        for p in range(2):
            exact = _swap_by_value_then_index
            for grp in range(nk // k):
                items = [(st_ref[slot, p, pl.ds(grp * k + e, SUBLANES, stride=STAGE_PITCH), :],
                          jnp.full((SUBLANES, LANES), float(grp * k + e), f32)) for e in range(k)]
                for e, (v, idx) in enumerate(_run_net(items, _batcher_pairs(k), exact)):
                    lv_ref[grp, e] = v
                    li_ref[grp, e] = idx
            span = 1
            while span < nk // k:
                for ga in range(0, nk // k, 2 * span):
                    gb = ga + span
                    top = _merge_top([(lv_ref[ga, e], li_ref[ga, e]) for e in range(k)],
                                     [(lv_ref[gb, e], li_ref[gb, e]) for e in range(k)], exact)
                    for e, (v, idx) in enumerate(top):
                        if 2 * span < nk // k:
                            lv_ref[ga, e] = v
                            li_ref[ga, e] = idx
                        else:
                            ts_ref[p, e] = v
                            ti_ref[p, e] = idx
                span *= 2

        best, loser = _top_pairs(lambda a, b: (ts_ref[0, a] + ts_ref[1, b], ti_ref[0, a], ti_ref[1, b]), neg)
        tie = jnp.maximum(tie, (best[k - 1][0] == loser).astype(f32))
        es = [jnp.exp(v - best[0][0]) for v, _, _ in best]
        den = functools.reduce(lambda u, w: u + w, es)
        for e in range(k):
            r0 = pl.multiple_of((h * k + e) * SUBLANES, SUBLANES)
            oi_ref[pl.ds(r0, SUBLANES), :] = best[e][1]
            oj_ref[pl.ds(r0, SUBLANES), :] = best[e][2]
            og_ref[pl.ds(r0, SUBLANES), :] = es[e] / den
        scores(jnp.minimum(h + 1, PEER_HEADS - 1), 1 - slot)
        return tie

    tie = lax.fori_loop(0, PEER_HEADS, head, jnp.zeros((SUBLANES, LANES), f32))
    for c in range(tt // LANES):
        rows = slice(c * LANES, (c + 1) * LANES)
        i_ref[0, rows, :] = oi_ref[pl.ds(c, PEER_HK, stride=SUBLANES), :].T
        j_ref[0, rows, :] = oj_ref[pl.ds(c, PEER_HK, stride=SUBLANES), :].T
        gate_ref[0, rows, :] = og_ref[pl.ds(c, PEER_HK, stride=SUBLANES), :].T

    tie_ref[...] = tie
    rows_per_chunk = EXACT_CHUNK // LANES

    def chunk(ci, carry):
        flagged = jnp.max(tie_ref[pl.ds(ci * rows_per_chunk, rows_per_chunk), :]) > 0.0

        @pl.when(flagged)
        def _exact():
            r0 = pl.multiple_of(ci * EXACT_CHUNK, EXACT_CHUNK)
            hc = _rms_mod(x_ref[0, pl.ds(r0, EXACT_CHUNK), :], g_ref[...], sh_ref[0], sc_ref[0])
            ii, jj, gg = _route_exact_chunk(hc, wq_ref, kd_ref)
            i_ref[0, pl.ds(r0, EXACT_CHUNK), :] = ii
            j_ref[0, pl.ds(r0, EXACT_CHUNK), :] = jj
            gate_ref[0, pl.ds(r0, EXACT_CHUNK), :] = gg

        return carry

    lax.fori_loop(0, tt // EXACT_CHUNK, chunk, 0)


def _route(x, g, sh, sc, wq, kd):
    b, s, d = x.shape
    tt = ROUTE_TILE
    k = PEER_TOPK
    tok = lambda w: pl.BlockSpec((1, tt, w), lambda i, j: (i, j, 0))
    entry = (SUBLANES, LANES)
    return pl.pallas_call(
        functools.partial(_route_kernel, tt),
        grid=(b, s // tt),
        in_specs=[tok(d), _const_spec((1, d)),
                  pl.BlockSpec((1, 1, d), lambda i, j: (i, 0, 0)),
                  pl.BlockSpec((1, 1, d), lambda i, j: (i, 0, 0)),
                  _const_spec(wq.shape), _const_spec(kd.shape)],
        out_specs=[tok(d), tok(PEER_HK), tok(PEER_HK), tok(PEER_HK)],
        out_shape=[jax.ShapeDtypeStruct((b, s, d), bf16)] + [jax.ShapeDtypeStruct((b, s, PEER_HK), f32)] * 3,
        scratch_shapes=[pltpu.VMEM((tt, d), bf16),
                        pltpu.VMEM((2, 2, (tt // LANES) * STAGE_PITCH, LANES), f32),
                        pltpu.VMEM((PEER_NKEYS // k, k) + entry, f32), pltpu.VMEM((PEER_NKEYS // k, k) + entry, f32),
                        pltpu.VMEM((2, k) + entry, f32), pltpu.VMEM((2, k) + entry, f32),
                        pltpu.VMEM((PEER_HK * SUBLANES, LANES), f32), pltpu.VMEM((PEER_HK * SUBLANES, LANES), f32),
                        pltpu.VMEM((PEER_HK * SUBLANES, LANES), f32), pltpu.VMEM(entry, f32)],
        compiler_params=_params(("parallel", "parallel"), 56),
        name="peer_route",
    )(x, g.reshape(1, d), sh, sc, wq, kd)


EXPERT_BLOCK = 2048
I_PER_BLOCK = EXPERT_BLOCK // PEER_NKEYS
N_EXPERT_BLOCKS = PEER_NKEYS * PEER_NKEYS // EXPERT_BLOCK
G_PITCH_PAD = SUBLANES
BUILD_PAIRS = 16
ROW_SPLITS = 2


def _two_gelu(a):
    return a * (1.0 + lax.erf(a * (2.0 ** -0.5)))


def _bf16_bits(a):
    return lax.bitcast_convert_type(a.astype(bf16).astype(f32), jnp.uint32)


def _expert_kernel(tt, final, x_ref, h2_ref, i_ref, j_ref, gate_ref, gt_ref, ut_ref, v_ref, fg_ref, o_ref, g_buf, acc_ref):
    ib = pl.program_id(2)
    pitch = tt // 2 + G_PITCH_PAD

    @pl.when(ib == 0)
    def _build():
        sub = lax.broadcasted_iota(jnp.int32, (PEER_NKEYS, PEER_HK), 0).astype(f32).astype(bf16)
        one = jnp.ones((), bf16)
        zero = jnp.zeros((), bf16)

        def token_grid(t):
            irow = i_ref[0, pl.ds(t, 1), :].astype(bf16)
            jrow = j_ref[0, pl.ds(t, 1), :].astype(bf16)
            grow = (0.5 * gate_ref[0, pl.ds(t, 1), :]).astype(bf16)
            pt = jnp.where(sub == irow, one, zero)
            qt = jnp.where(sub == jrow, grow, zero)
            return _bf16_bits(_dot_nt(pt, qt))

        def body(step, carry):
            for u in range(BUILD_PAIRS):
                w = step * BUILD_PAIRS + u
                packed = lax.bitwise_or(token_grid(2 * w + 1), lax.shift_right_logical(token_grid(2 * w), jnp.uint32(16)))
                g_buf[pl.ds(w, PEER_NKEYS, stride=pitch), :] = packed
            return carry

        lax.fori_loop(0, tt // (2 * BUILD_PAIRS), body, 0)
        acc_ref[...] = jnp.zeros_like(acc_ref)

    rows = tt // ROW_SPLITS
    for r in range(0, tt, rows):
        a = _dot(h2_ref[0, r:r + rows, :], ut_ref[...]).astype(bf16)
        pieces = []
        for k in range(I_PER_BLOCK):
            row0 = pl.multiple_of((ib * I_PER_BLOCK + k) * pitch + r // 2, SUBLANES)
            gk = pltpu.bitcast(g_buf[pl.ds(row0, rows // 2), :], bf16)
            pieces.append(gk * _two_gelu(a[:, k * PEER_NKEYS:(k + 1) * PEER_NKEYS]))
        acc_ref[r:r + rows, :] += _dot(jnp.concatenate(pieces, axis=1), v_ref[...])

    @pl.when(ib == N_EXPERT_BLOCKS - 1)
    def _finish():
        y = x_ref[0] + gt_ref[0] * acc_ref[...]
        if final:
            ms = jnp.mean(y * y, axis=-1, keepdims=True)
            y = y * lax.rsqrt(ms + NORM_EPS) * fg_ref[...]
        o_ref[0] = y


def _experts(x, h2, idx_i, idx_j, gate, gt, ut, v, layer, final_g, final, tt):
    b, s, d = x.shape
    tok = lambda w: pl.BlockSpec((1, tt, w), lambda i, j, e: (i, j, 0))
    return pl.pallas_call(
        functools.partial(_expert_kernel, tt, final),
        grid=(b, s // tt, N_EXPERT_BLOCKS),
        in_specs=[tok(d), tok(d), tok(PEER_HK), tok(PEER_HK), tok(PEER_HK),
                  pl.BlockSpec((1, 1, d), lambda i, j, e: (i, 0, 0)),
                  pl.BlockSpec((None, d, EXPERT_BLOCK), lambda i, j, e: (layer, 0, e)),
                  pl.BlockSpec((None, EXPERT_BLOCK, d), lambda i, j, e: (layer, e, 0)),
                  pl.BlockSpec((1, d), lambda i, j, e: (0, 0))],
        out_specs=tok(d),
        out_shape=jax.ShapeDtypeStruct((b, s, d), f32),
        scratch_shapes=[pltpu.VMEM((PEER_NKEYS * (tt // 2 + G_PITCH_PAD), PEER_NKEYS), jnp.uint32),
                        pltpu.VMEM((tt, d), f32)],
        compiler_params=_params(("parallel", "parallel", "arbitrary"), 56),
        name="peer_experts",
    )(x, h2, idx_i, idx_j, gate, gt, ut, v, final_g.reshape(1, d))


def _rope_tables(seq):
    t = jnp.arange(seq)
    pos = jnp.stack([t // GRID_W, t % GRID_W], axis=1).astype(f32)
    f = HEAD_DIM // 4
    inv = ROPE_BASE ** (-jnp.arange(f, dtype=f32) / f)
    lane = jnp.arange(LANES)
    l64 = lane % HEAD_DIM
    ang = pos[:, (l64 // (2 * f))] * inv[l64 % f][None, :]
    first = (l64 % (2 * f)) < f
    cos = jnp.cos(ang)
    sin = jnp.sin(ang)
    return cos, jnp.where(first[None, :], -sin, 0.0), jnp.where(first[None, :], 0.0, sin)


def _na_bias_table(rpb, rows):
    r0 = np.array(_na_class_rows(rows))[:, None, None]
    a = np.arange(NA_ROWS)[None, :, None]
    kr = np.arange(NA_SPAN)[None, None, :]
    r = r0 + a
    ws = np.clip(r - NA_KH // 2, 0, rows - NA_KH)
    absrow = np.clip(r0 - NA_KH // 2, 0, rows - NA_SPAN) + kr
    ok_r = (absrow >= ws) & (absrow < ws + NA_KH)
    dr = absrow - r + NA_KH - 1
    oh_r = (ok_r[..., None] & (dr[..., None] == np.arange(2 * NA_KH - 1))).astype(np.float32)
    qc = np.arange(GRID_W)[:, None]
    kc = np.arange(GRID_W)[None, :]
    cstart = np.clip(qc - NA_KW // 2, 0, GRID_W - NA_KW)
    ok_c = (kc >= cstart) & (kc < cstart + NA_KW)
    dc = kc - qc + NA_KW - 1
    oh_c = (ok_c[..., None] & (dc[..., None] == np.arange(2 * NA_KW - 1))).astype(np.float32)
    hp = lax.Precision.HIGHEST
    t1 = jnp.einsum("hde,qke->hdqk", rpb.astype(f32), jnp.asarray(oh_c), precision=hp)
    bias = jnp.einsum("cafd,hdqk->hcaqfk", jnp.asarray(oh_r), t1, precision=hp)
    ok = ok_r[None, :, :, None, :, None] & ok_c[None, None, None, :, None, :]
    bias = jnp.where(jnp.asarray(ok), bias, NEG_BIG)
    return bias.reshape(NA_HEADS, NA_CLASSES, NA_ROWS * GRID_W, NA_SPAN * GRID_W)


def _layout_w_in(w):
    c = lambda off, width: w[:, off:off + width]
    hd = HEAD_DIM
    k0, k1 = c(OFF_KW, hd), c(OFF_KW + hd, hd)
    v0, v1 = c(OFF_VW, hd), c(OFF_VW + hd, hd)
    cols = [c(OFF_KN, NA_WIDTH), c(OFF_VN, NA_WIDTH), k0, k0, k1, k1, v0, v0, v1, v1,
            c(OFF_QN, NA_WIDTH), c(OFF_QW, SWA_Q_WIDTH), c(OFF_A, POOL_WIDTH), c(OFF_G, 3 * D_MODEL)]
    return jnp.concatenate(cols, axis=1).astype(bf16)


def _layout_peer_queries(wq):
    d = wq.shape[0]
    return wq.reshape(d, PEER_HEADS, 2 * PEER_NKEYS).transpose(1, 0, 2)


def _layout_peer_keys(keys):
    h, _, nk, dk = keys.shape
    z = jnp.zeros((h, nk, dk), keys.dtype)
    return jnp.concatenate([jnp.concatenate([keys[:, 0], z], axis=2), jnp.concatenate([z, keys[:, 1]], axis=2)], axis=1)


LATENT_TILE = 512
EXPERT_TILE = 512


def kernel(x, c, ctx, c_ctx, ada_w, ada_b, norm1_g, norm2_g, w_in, pool_w, pool_scale, na_rpb, swa_sink,
           w_branch_a, w_branch_b, w_branch_c, w_out, peer_wq, peer_keys, peer_u, peer_v, final_g):
    b, s, d = x.shape
    lc = ctx.shape[1]
    depth = ada_w.shape[0]
    cpad = jnp.concatenate([c, c_ctx[None, :], jnp.zeros((SUBLANES - b - 1, d), f32)], axis=0)
    mod = _modulation(cpad, ada_w, ada_b)
    rope_tabs = _rope_tables(s)
    ut = jnp.swapaxes(peer_u, 1, 2).astype(bf16)
    vb = peer_v.astype(bf16)
    xc = ctx
    for l in range(depth):
        last = l == depth - 1
        ml = mod[l, :b].reshape(b, 6, 1, d)
        sh1, sc1, gt1, sh2, sc2, gt2 = (ml[:, i] for i in range(6))
        mc = jnp.broadcast_to(mod[l, b].reshape(1, 6, 1, d), (b, 6, 1, d))
        csh1, csc1, cgt1, csh2, csc2, cgt2 = (mc[:, i] for i in range(6))

        w_aug = _layout_w_in(w_in[l])
        wa, wb, wc, wo = (w[l].astype(bf16) for w in (w_branch_a, w_branch_b, w_branch_c, w_out))
        pw = pool_w[l].astype(bf16)
        bias_tbl = _na_bias_table(na_rpb[l], s // GRID_W)
        wq = _layout_peer_queries(peer_wq[l]).astype(bf16)
        kd = _layout_peer_keys(peer_keys[l]).astype(bf16)

        knvn, sw, q, a, g = _inproj(x, norm1_g[l], sh1, sc1, w_aug, SEGS_ALL, rope_tabs, LATENT_TILE)
        if last:
            knvn_c, sw_c = _inproj(xc, norm1_g[l], csh1, csc1, w_aug, SEGS_KV, None, lc)
        else:
            knvn_c, sw_c, q_c, a_c, g_c = _inproj(xc, norm1_g[l], csh1, csc1, w_aug, SEGS_ALL, None, lc)
        ya = _pool(a, pw, pool_scale[l], 256)
        ob = _na(q, knvn, knvn_c, bias_tbl)
        oc = _swa(q, sw, sw_c, swa_sink[l])
        x = _merge(x, ya, ob, oc, g, gt1, wa, wb, wc, wo, LATENT_TILE)
        if not last:
            ya_c = _pool(a_c, pw, pool_scale[l], lc)
            ob_c, oc_c = _ctx_attn(q_c, knvn_c, sw_c, swa_sink[l])
            xc = _merge(xc, ya_c, ob_c, oc_c, g_c, cgt1, wa, wb, wc, wo, lc)

        h2, ii, jj, gate = _route(x, norm2_g[l], sh2, sc2, wq, kd)
        x = _experts(x, h2, ii, jj, gate, gt2, ut, vb, l, final_g, last, EXPERT_TILE)
        if not last:
            xf = xc.reshape(1, b * lc, d)
            h2c, iic, jjc, gatec = _route(xf, norm2_g[l], csh2[:1], csc2[:1], wq, kd)
            xc = _experts(xf, h2c, iic, jjc, gatec, cgt2[:1], ut, vb, l, final_g, False, EXPERT_TILE).reshape(b, lc, d)
    return x
```

```python
import functools

import jax
import jax.numpy as jnp
import numpy as np
from jax import lax
from jax.experimental import pallas as pl
from jax.experimental.pallas import tpu as pltpu

f32 = jnp.float32
bf16 = jnp.bfloat16

D_MODEL = 1024
GRID_W = 64
HEAD_DIM = 64
NORM_EPS = 1e-6
ROPE_BASE = 10000.0

POOL_WIDTH = 512
POOL_WINDOWS = (2, 4, 8, 16)
POOL_GROUP_W = 128
POOL_HALO = 8

NA_HEADS = 8
NA_KH = 8
NA_KW = 16

SWA_HEADS = 8
SWA_KV_HEADS = 2
SWA_WINDOW = 128
SWA_BLOCK = 128

PEER_HEADS = 8
PEER_NKEYS = 128
PEER_TOPK = 16
PEER_HK = PEER_HEADS * PEER_TOPK

NA_WIDTH = NA_HEADS * HEAD_DIM
SWA_Q_WIDTH = SWA_HEADS * HEAD_DIM
SWA_KV_WIDTH = SWA_KV_HEADS * HEAD_DIM
OFF_KN = 0
OFF_VN = OFF_KN + NA_WIDTH
OFF_KW = OFF_VN + NA_WIDTH
OFF_VW = OFF_KW + SWA_KV_WIDTH
OFF_QN = OFF_VW + SWA_KV_WIDTH
OFF_QW = OFF_QN + NA_WIDTH
OFF_A = OFF_QW + SWA_Q_WIDTH
OFF_G = OFF_A + POOL_WIDTH

LANES = 128
SUBLANES = 8
NEG_BIG = -1e30
MIB = 1024 * 1024

SEG_KN = ("knvn", 0, 1024)
SEG_SW = ("sw", 1024, 512)
SEG_Q = ("q", 1536, 1024)
SEG_A = ("a", 2560, 512)
SEG_G = ("g", 3072, 3072)
SEGS_ALL = (SEG_KN, SEG_SW, SEG_Q, SEG_A, SEG_G)
SEGS_KV = (SEG_KN, SEG_SW)
SEG_DTYPE = {"knvn": bf16, "sw": bf16, "q": bf16, "a": f32, "g": bf16}
COL_CHUNK = 512
ATTN_SCALE = HEAD_DIM ** -0.5
assert ATTN_SCALE == 0.125


def _params(sem, vmem_mib):
    return pltpu.CompilerParams(dimension_semantics=sem, vmem_limit_bytes=vmem_mib * MIB)


def _const_spec(shape):
    nd = len(shape)
    return pl.BlockSpec(shape, lambda *_: (0,) * nd, pipeline_mode=pl.Buffered(1))


def _dot(a, b):
    return jnp.dot(a, b, preferred_element_type=f32)


def _dot_nt(a, b):
    return lax.dot_general(a, b, (((1,), (1,)), ((), ())), preferred_element_type=f32)


def _split(a):
    hi = a.astype(bf16)
    lo = (a - hi.astype(f32)).astype(bf16)
    return hi, lo


def _sigmoid(z):
    return 1.0 / (1.0 + jnp.exp(-z))


def _rms_mod(x, g, sh, sc):
    ms = jnp.mean(x * x, axis=-1, keepdims=True)
    h = x * lax.rsqrt(ms + NORM_EPS) * g
    return h * (1.0 + sc) + sh


def _mod_kernel(c_ref, w_ref, b_ref, o_ref):
    c = c_ref[...]
    s = c * _sigmoid(c)
    sh, sl = _split(s)
    wh, wl = _split(w_ref[0])
    o_ref[0] = _dot(sh, wh) + _dot(sh, wl) + _dot(sl, wh) + b_ref[0]


def _modulation(cpad, ada_w, ada_b):
    depth, d, n = ada_w.shape
    tn = 1536
    return pl.pallas_call(
        _mod_kernel,
        grid=(depth, n // tn),
        in_specs=[
            pl.BlockSpec((SUBLANES, d), lambda l, j: (0, 0)),
            pl.BlockSpec((1, d, tn), lambda l, j: (l, 0, j)),
            pl.BlockSpec((1, 1, tn), lambda l, j: (l, 0, j)),
        ],
        out_specs=pl.BlockSpec((1, SUBLANES, tn), lambda l, j: (l, 0, j)),
        out_shape=jax.ShapeDtypeStruct((depth, SUBLANES, n), f32),
        compiler_params=_params(("arbitrary", "arbitrary"), 40),
        name="adaln_mod",
    )(cpad, ada_w, ada_b.reshape(depth, 1, n))


def _rope(z, cos, s1, s2):
    return z * cos + pltpu.roll(z, LANES - 16, 1) * s1 + pltpu.roll(z, 16, 1) * s2


def _inproj_kernel(segs, rope, x_ref, g_ref, sh_ref, sc_ref, w_ref, *rest):
    if rope:
        cos_ref, s1_ref, s2_ref = rest[:3]
        outs = rest[3:]
        cos, s1, s2 = cos_ref[...], s1_ref[...], s2_ref[...]
    else:
        outs = rest
    hb = _rms_mod(x_ref[0], g_ref[...], sh_ref[0], sc_ref[0]).astype(bf16)
    for (name, c0, width), o_ref in zip(segs, outs):
        for j in range(0, width, COL_CHUNK):
            z = _dot(hb, w_ref[:, c0 + j:c0 + j + COL_CHUNK])
            if name == "g":
                z = _sigmoid(z)
            if name == "q":
                z = z * ATTN_SCALE
            if rope and name in ("sw", "q"):
                n_rot = 2 if name == "sw" else (4 if j == COL_CHUNK else 0)
                slabs = [z[:, s * LANES:(s + 1) * LANES] for s in range(COL_CHUNK // LANES)]
                slabs = [_rope(sl, cos, s1, s2) if s < n_rot else sl for s, sl in enumerate(slabs)]
                z = jnp.concatenate(slabs, axis=1)
            o_ref[0, :, j:j + COL_CHUNK] = z.astype(o_ref.dtype)


def _inproj(x, g, sh, sc, w_aug, layer, segs, rope_tabs, tm):
    b, s, d = x.shape
    rope = rope_tabs is not None
    in_specs = [
        pl.BlockSpec((1, tm, d), lambda i, j: (i, j, 0)),
        _const_spec((1, d)),
        pl.BlockSpec((1, 1, d), lambda i, j: (i, 0, 0)),
        pl.BlockSpec((1, 1, d), lambda i, j: (i, 0, 0)),
        pl.BlockSpec((None,) + w_aug.shape[1:], lambda i, j: (layer, 0, 0), pipeline_mode=pl.Buffered(1)),
    ]
    args = [x, g.reshape(1, d), sh, sc, w_aug]
    if rope:
        in_specs += [pl.BlockSpec((tm, LANES), lambda i, j: (j, 0))] * 3
        args += list(rope_tabs)
    out_specs = [pl.BlockSpec((1, tm, w), lambda i, j: (i, j, 0)) for _, _, w in segs]
    out_shape = [jax.ShapeDtypeStruct((b, s, w), SEG_DTYPE[n]) for n, _, w in segs]
    return pl.pallas_call(
        functools.partial(_inproj_kernel, segs, rope),
        grid=(b, s // tm),
        in_specs=in_specs,
        out_specs=out_specs,
        out_shape=out_shape,
        compiler_params=_params(("parallel", "parallel"), 52),
        name="inproj",
    )(*args)


def _pool_kernel(seq, tt, a_ref, pw_ref, ps_ref, o_ref):
    t0 = pl.multiple_of(pl.program_id(1) * tt, tt)
    cur = a_ref[0, pl.ds(t0, tt), :]
    p0 = pl.multiple_of(jnp.maximum(t0 - POOL_HALO, 0), POOL_HALO)
    n0 = pl.multiple_of(jnp.minimum(t0 + tt, seq - POOL_HALO), POOL_HALO)
    prev = jnp.where(t0 > 0, a_ref[0, pl.ds(p0, POOL_HALO), :], 0.0)
    nxt = jnp.where(t0 + tt < seq, a_ref[0, pl.ds(n0, POOL_HALO), :], 0.0)
    ext = jnp.concatenate([prev, cur, nxt], axis=0)
    n_ext = tt + 2 * POOL_HALO
    tpos = t0 + lax.broadcasted_iota(jnp.int32, (tt, 1), 0)
    for g, w in enumerate(POOL_WINDOWS):
        sl = slice(g * POOL_GROUP_W, (g + 1) * POOL_GROUP_W)
        eg = ext[:, sl]
        acc = jnp.zeros((tt, POOL_GROUP_W), f32)
        for k in range(-(w // 2), w - w // 2):
            shifted = eg if k == 0 else pltpu.roll(eg, (-k) % n_ext, 0)
            acc = acc + shifted[POOL_HALO:POOL_HALO + tt]
        lo = jnp.clip(tpos - w // 2, 0, seq)
        hi = jnp.clip(tpos - w // 2 + w, 0, seq)
        pooled = acc / (hi - lo).astype(f32) - cur[:, sl]
        y = _dot(pooled.astype(bf16), pw_ref[g]) * ps_ref[:, sl]
        o_ref[0, :, sl] = y.astype(o_ref.dtype)


def _pool(a, pool_w, pool_scale, tt):
    b, s, w = a.shape
    return pl.pallas_call(
        functools.partial(_pool_kernel, s, tt),
        grid=(b, s // tt),
        in_specs=[
            pl.BlockSpec((1, s, w), lambda i, j: (i, 0, 0)),
            _const_spec(pool_w.shape),
            _const_spec((1, w)),
        ],
        out_specs=pl.BlockSpec((1, tt, w), lambda i, j: (i, j, 0)),
        out_shape=jax.ShapeDtypeStruct((b, s, w), bf16),
        compiler_params=_params(("parallel", "arbitrary"), 40),
        name="pool_mixer",
    )(a, pool_w, pool_scale.reshape(1, w))


def _half_masks():
    lane = lax.broadcasted_iota(jnp.int32, (1, LANES), 1)
    lo = lane < HEAD_DIM
    return lo, jnp.logical_not(lo)


def _attend_separate(logits, vals, extra_logit):
    m = functools.reduce(jnp.maximum, [jnp.max(s, axis=-1, keepdims=True) for s in logits])
    if extra_logit is not None:
        m = jnp.maximum(m, extra_logit)
    den = None
    out = None
    for s, v in zip(logits, vals):
        e = jnp.exp(s - m)
        d = jnp.sum(e, axis=-1, keepdims=True)
        o = _dot(e.astype(bf16), v)
        den = d if den is None else den + d
        out = o if out is None else out + o
    if extra_logit is not None:
        den = den + jnp.exp(extra_logit - m)
    return out / den


def _attend(qm, keys, vals, biases, extra_logit):
    logits = []
    for k, bias in zip(keys, biases):
        s = _dot_nt(qm, k)
        if bias is not None:
            s = s + bias
        logits.append(s)
    if any(k.shape[0] % LANES for k in keys):
        return _attend_separate(logits, vals, extra_logit)
    s = jnp.concatenate(logits, axis=-1)
    m = jnp.max(s, axis=-1, keepdims=True)
    if extra_logit is not None:
        m = jnp.maximum(m, extra_logit)
    e = jnp.exp(s - m)
    den = jnp.sum(e, axis=-1, keepdims=True)
    if extra_logit is not None:
        den = den + jnp.exp(extra_logit - m)
    eb = e.astype(bf16)
    out = None
    c0 = 0
    for k, v in zip(keys, vals):
        o = _dot(eb[:, c0:c0 + k.shape[0]], v)
        out = o if out is None else out + o
        c0 += k.shape[0]
    return out / den


NA_ROWS = 2
NA_SPAN = NA_KH + NA_ROWS - 1
NA_CLASSES = 5


def _na_class_rows(rows):
    return (0, 2, 4, rows - 4, rows - 2)


def _na_kernel(rows, q_ref, k_ref, v_ref, kc_ref, vc_ref, bias_ref, o_ref):
    r0 = pl.program_id(1) * NA_ROWS
    k0 = pl.multiple_of(jnp.clip(r0 - NA_KH // 2, 0, rows - NA_SPAN) * GRID_W, GRID_W)
    nk = NA_SPAN * GRID_W
    nq = NA_ROWS * GRID_W
    lo, hi = _half_masks()
    for p in range(NA_HEADS // 2):
        sl = slice(p * LANES, (p + 1) * LANES)
        qp = q_ref[0, :, sl]
        zero = jnp.zeros_like(qp)
        qs = jnp.concatenate([jnp.where(lo, qp, zero), jnp.where(hi, qp, zero)], axis=0)
        bias = jnp.concatenate([bias_ref[2 * p, 0], bias_ref[2 * p + 1, 0]], axis=0)
        o = _attend(qs, (kc_ref[0, :, sl], k_ref[0, pl.ds(k0, nk), sl]),
                    (vc_ref[0, :, sl], v_ref[0, pl.ds(k0, nk), sl]), (None, bias), None)
        o_ref[0, :, sl] = jnp.where(lo, o[:nq], o[nq:]).astype(o_ref.dtype)


def _na(q, knvn, knvn_c, bias_tbl):
    b, s, _ = q.shape
    rows = s // GRID_W
    lc = knvn_c.shape[1]
    nq = NA_ROWS * GRID_W

    def cls(i):
        r0 = i * NA_ROWS
        return jnp.where(r0 < 4, r0 // 2, jnp.where(r0 >= rows - 4, 3 + (r0 - (rows - 4)) // 2, 2))

    return pl.pallas_call(
        functools.partial(_na_kernel, rows),
        grid=(b, rows // NA_ROWS),
        in_specs=[
            pl.BlockSpec((1, nq, NA_WIDTH), lambda i, r: (i, r, 0)),
            pl.BlockSpec((1, s, NA_WIDTH), lambda i, r: (i, 0, 0)),
            pl.BlockSpec((1, s, NA_WIDTH), lambda i, r: (i, 0, 1)),
            pl.BlockSpec((1, lc, NA_WIDTH), lambda i, r: (i, 0, 0)),
            pl.BlockSpec((1, lc, NA_WIDTH), lambda i, r: (i, 0, 1)),
            pl.BlockSpec((NA_HEADS, 1, nq, NA_SPAN * GRID_W), lambda i, r: (0, cls(r), 0, 0)),
        ],
        out_specs=pl.BlockSpec((1, nq, NA_WIDTH), lambda i, r: (i, r, 0)),
        out_shape=jax.ShapeDtypeStruct((b, s, NA_WIDTH), bf16),
        compiler_params=_params(("parallel", "arbitrary"), 48),
        name="nbr_attn",
    )(q, knvn, knvn, knvn_c, knvn_c, bias_tbl)


def _swa_kernel(seq, q_ref, sw_ref, swc_ref, sink_ref, o_ref):
    n = pl.program_id(1)
    span = 3 * SWA_BLOCK
    start = pl.multiple_of(jnp.clip((n - 1) * SWA_BLOCK, 0, seq - span), SWA_BLOCK)
    qpos = n * SWA_BLOCK + lax.broadcasted_iota(jnp.int32, (SWA_BLOCK, 1), 0)
    kpos = start + lax.broadcasted_iota(jnp.int32, (1, span), 1)
    bias1 = jnp.where(jnp.abs(qpos - kpos) <= SWA_WINDOW, 0.0, NEG_BIG)
    lo, hi = _half_masks()
    rep = SWA_HEADS // SWA_KV_HEADS
    bias = jnp.concatenate([bias1] * rep, axis=0)
    for g in range(SWA_KV_HEADS):
        kd = sw_ref[0, pl.ds(start, span), g * LANES:(g + 1) * LANES]
        vd = sw_ref[0, pl.ds(start, span), (2 + g) * LANES:(3 + g) * LANES]
        kcd = swc_ref[0, :, g * LANES:(g + 1) * LANES]
        vcd = swc_ref[0, :, (2 + g) * LANES:(3 + g) * LANES]
        qs, sinks = [], []
        for h in range(g * rep, (g + 1) * rep):
            qp = q_ref[0, :, (h // 2) * LANES:(h // 2 + 1) * LANES]
            qs.append(jnp.where(hi if h % 2 else lo, qp, jnp.zeros_like(qp)))
            sinks.append(jnp.full((SWA_BLOCK, 1), sink_ref[h], f32))
        o = _attend(jnp.concatenate(qs, axis=0), (kd, kcd), (vd, vcd), (bias, None), jnp.concatenate(sinks, axis=0))
        for j in range(rep // 2):
            p = (g * rep) // 2 + j
            o_ref[0, :, p * LANES:(p + 1) * LANES] = jnp.where(
                lo, o[2 * j * SWA_BLOCK:(2 * j + 1) * SWA_BLOCK], o[(2 * j + 1) * SWA_BLOCK:(2 * j + 2) * SWA_BLOCK]
            ).astype(o_ref.dtype)


def _swa(q, sw, sw_c, sink):
    b, s, _ = q.shape
    lc = sw_c.shape[1]
    return pl.pallas_call(
        functools.partial(_swa_kernel, s),
        grid=(b, s // SWA_BLOCK),
        in_specs=[
            pl.BlockSpec((1, SWA_BLOCK, SWA_Q_WIDTH), lambda i, n: (i, n, 1)),
            pl.BlockSpec((1, s, 512), lambda i, n: (i, 0, 0)),
            pl.BlockSpec((1, lc, 512), lambda i, n: (i, 0, 0)),
            pl.BlockSpec(memory_space=pltpu.SMEM),
        ],
        out_specs=pl.BlockSpec((1, SWA_BLOCK, SWA_Q_WIDTH), lambda i, n: (i, n, 0)),
        out_shape=jax.ShapeDtypeStruct((b, s, SWA_Q_WIDTH), bf16),
        compiler_params=_params(("parallel", "arbitrary"), 40),
        name="win_attn",
    )(q, sw, sw_c, sink)


def _ctx_attn_kernel(q_ref, kv_ref, sw_ref, sink_ref, ob_ref, oc_ref):
    lo, hi = _half_masks()
    rep = SWA_HEADS // SWA_KV_HEADS
    for p in range(NA_HEADS // 2):
        sl = slice(p * LANES, (p + 1) * LANES)
        qp = q_ref[0, :, sl]
        kp = kv_ref[0, :, sl]
        vp = kv_ref[0, :, NA_WIDTH + p * LANES:NA_WIDTH + (p + 1) * LANES]
        halves = [_attend(jnp.where(msk, qp, jnp.zeros_like(qp)), (kp,), (vp,), (None,), None) for msk in (lo, hi)]
        ob_ref[0, :, sl] = jnp.where(lo, halves[0], halves[1]).astype(ob_ref.dtype)
    outs = []
    for h in range(SWA_HEADS):
        g = h // rep
        kd = sw_ref[0, :, g * LANES:(g + 1) * LANES]
        vd = sw_ref[0, :, (2 + g) * LANES:(3 + g) * LANES]
        qp = q_ref[0, :, NA_WIDTH + (h // 2) * LANES:NA_WIDTH + (h // 2 + 1) * LANES]
        qm = jnp.where(hi if h % 2 else lo, qp, jnp.zeros_like(qp))
        outs.append(_attend(qm, (kd,), (vd,), (None,), sink_ref[h]))
    for p in range(SWA_HEADS // 2):
        oc_ref[0, :, p * LANES:(p + 1) * LANES] = jnp.where(lo, outs[2 * p], outs[2 * p + 1]).astype(oc_ref.dtype)


def _ctx_attn(q_c, knvn_c, sw_c, sink):
    b, lc, _ = q_c.shape
    spec = lambda w: pl.BlockSpec((1, lc, w), lambda i: (i, 0, 0))
    return pl.pallas_call(
        _ctx_attn_kernel,
        grid=(b,),
        in_specs=[spec(1024), spec(1024), spec(512), pl.BlockSpec(memory_space=pltpu.SMEM)],
        out_specs=[spec(NA_WIDTH), spec(SWA_Q_WIDTH)],
        out_shape=[jax.ShapeDtypeStruct((b, lc, NA_WIDTH), bf16), jax.ShapeDtypeStruct((b, lc, SWA_Q_WIDTH), bf16)],
        compiler_params=_params(("parallel",), 40),
        name="ctx_attn",
    )(q_c, knvn_c, sw_c, sink)


def _merge_kernel(x_ref, ya_ref, ob_ref, oc_ref, g_ref, gt_ref, wa_ref, wb_ref, wc_ref, wo_ref, o_ref):
    d = D_MODEL
    m = g_ref[0, :, 0:d].astype(f32) * _dot(ya_ref[0], wa_ref[...])
    m = m + g_ref[0, :, d:2 * d].astype(f32) * _dot(ob_ref[0], wb_ref[...])
    m = m + g_ref[0, :, 2 * d:3 * d].astype(f32) * _dot(oc_ref[0], wc_ref[...])
    o_ref[0] = x_ref[0] + gt_ref[0] * _dot(m.astype(bf16), wo_ref[...])


def _merge(x, ya, ob, oc, g, gt, wa, wb, wc, wo, tm):
    b, s, d = x.shape
    tok = lambda w: pl.BlockSpec((1, tm, w), lambda i, j: (i, j, 0))
    return pl.pallas_call(
        _merge_kernel,
        grid=(b, s // tm),
        in_specs=[tok(d), tok(512), tok(512), tok(512), tok(3 * d),
                  pl.BlockSpec((1, 1, d), lambda i, j: (i, 0, 0)),
                  _const_spec(wa.shape), _const_spec(wb.shape), _const_spec(wc.shape), _const_spec(wo.shape)],
        out_specs=tok(d),
        out_shape=jax.ShapeDtypeStruct((b, s, d), f32),
        compiler_params=_params(("parallel", "parallel"), 48),
        name="merge_out",
    )(x, ya, ob, oc, g, gt, wa, wb, wc, wo)


def _topk_rows(s, k):
    n = s.shape[0]
    rid = lax.broadcasted_iota(jnp.int32, s.shape, 0)
    vals, idxs = [], []
    for _ in range(k):
        m = jnp.max(s, axis=0, keepdims=True)
        i = jnp.min(jnp.where(s == m, rid, n), axis=0, keepdims=True)
        vals.append(m)
        idxs.append(i)
        s = jnp.where(rid == i, -jnp.inf, s)
    return jnp.concatenate(vals, axis=0), jnp.concatenate(idxs, axis=0)


def _pick_rows(tab, sel):
    rid = lax.broadcasted_iota(jnp.int32, tab.shape, 0)
    rows = []
    for k in range(sel.shape[0]):
        rows.append(jnp.sum(jnp.where(rid == sel[k:k + 1, :], tab, 0.0), axis=0, keepdims=True))
    return jnp.concatenate(rows, axis=0)


def _route_exact_chunk(h2, wq_ref, kd_ref):
    hb = h2.astype(bf16)
    k = PEER_TOPK
    nk = PEER_NKEYS
    i_rows, j_rows, g_rows = [], [], []
    for h in range(PEER_HEADS):
        top_s, top_i = [], []
        for p in range(2):
            cs = slice(p * nk, (p + 1) * nk)
            q = _dot(hb, wq_ref[h, :, cs])
            st = _dot_nt(kd_ref[h, cs, cs], q.astype(bf16))
            ts, ti = _topk_rows(st, k)
            top_s.append(ts)
            top_i.append(ti.astype(f32))
        blocks = [top_s[0][0:1] + top_s[1][0:8], top_s[0][0:1] + top_s[1][8:16]]
        blocks += [top_s[0][a:a + 1] + top_s[1][0:8] for a in range(1, 8)]
        blocks += [top_s[0][8:16] + top_s[1][0:1]]
        best_s, row = _topk_rows(jnp.concatenate(blocks, axis=0), k)
        blk = lax.shift_right_logical(row, 3)
        sub = lax.bitwise_and(row, 7)
        tail = blk == len(blocks) - 1
        sel_a = jnp.where(tail, sub + 8, jnp.maximum(blk - 1, 0))
        sel_b = jnp.where(tail, 0, jnp.where(blk == 1, sub + 8, sub))
        i_rows.append(_pick_rows(top_i[0], sel_a))
        j_rows.append(_pick_rows(top_i[1], sel_b))
        e = jnp.exp(best_s - best_s[0:1, :])
        g_rows.append(e / jnp.sum(e, axis=0, keepdims=True))
    return (jnp.concatenate(i_rows, axis=0).T, jnp.concatenate(j_rows, axis=0).T, jnp.concatenate(g_rows, axis=0).T)


ROUTE_TILE = SUBLANES * LANES
STAGE_PITCH = PEER_NKEYS + SUBLANES
EXACT_CHUNK = 256


def _batcher_pairs(n):
    pairs, p = [], 1
    while p < n:
        k = p
        while k >= 1:
            for j in range(k % p, n - k, 2 * k):
                for i in range(min(k, n - j - k)):
                    if (i + j) // (2 * p) == (i + j + k) // (2 * p):
                        pairs.append((i + j, i + j + k))
            k //= 2
        p *= 2
    return tuple(pairs)


def _bitonic_pairs(n):
    pairs, j = [], n // 2
    while j >= 1:
        pairs += [(i, i + j) for i in range(n) if (i & j) == 0]
        j //= 2
    return tuple(pairs)


def _swap_by_value(x, y):
    return y[0] > x[0]


def _swap_by_value_then_index(x, y):
    return jnp.logical_or(y[0] > x[0], jnp.logical_and(y[0] == x[0], y[1] < x[1]))


def _cmpx(x, y, swap_fn, want_lo=True):
    swap = swap_fn(x, y)
    hi = tuple(jnp.where(swap, b, a) for a, b in zip(x, y))
    lo = tuple(jnp.where(swap, a, b) for a, b in zip(x, y)) if want_lo else None
    return hi, lo


def _run_net(items, pairs, swap_fn):
    items = list(items)
    for i, j in pairs:
        a, b = items[i], items[j]
        if b is None:
            continue
        if a is None:
            items[i], items[j] = b, None
        else:
            items[i], items[j] = _cmpx(a, b, swap_fn)
    return items


def _merge_all(a, b, swap_fn):
    return _run_net(list(a) + list(b)[::-1], _bitonic_pairs(2 * len(a)), swap_fn)


def _merge_top(a, b, swap_fn, dropped=None):
    m = len(a)
    top = []
    for k in range(m):
        x, y = a[k], b[m - 1 - k]
        if x is None or y is None:
            top.append(y if x is None else x)
        else:
            hi, lo = _cmpx(x, y, swap_fn, want_lo=dropped is not None)
            top.append(hi)
            if dropped is not None:
                dropped[0] = jnp.maximum(dropped[0], lo[0])
    return _run_net(top, _bitonic_pairs(m), swap_fn)


def _top_pairs(pair, neg):
    def row(a, n, width):
        return [pair(a, b) for b in range(n)] + [None] * (width - n)

    dropped = [neg]
    by_value = _swap_by_value
    x1 = _merge_all(row(1, 8, 8), [pair(a, 0) for a in range(8, 16)], by_value)
    x2 = _merge_all(row(2, 5, 8), row(3, 4, 8), by_value)
    x3 = _merge_all(row(4, 3, 4), row(5, 2, 4), by_value)
    x4 = _merge_all(row(6, 2, 2), row(7, 2, 2), by_value)
    x5 = _merge_all(x3, x4 + [None] * 4, by_value)
    y1 = _merge_top(row(0, 16, 16), x1, by_value, dropped)
    y2 = _merge_top(x2, x5, by_value, dropped)
    return _merge_top(y1, y2, by_value, dropped), dropped[0]


def _route_kernel(tt, x_ref, g_ref, sh_ref, sc_ref, wq_ref, kd_ref,
                  h2_ref, i_ref, j_ref, gate_ref,
                  hb_ref, st_ref, lv_ref, li_ref, ts_ref, ti_ref, oi_ref, oj_ref, og_ref, tie_ref):
    k = PEER_TOPK
    nk = PEER_NKEYS
    hb = _rms_mod(x_ref[0], g_ref[...], sh_ref[0], sc_ref[0]).astype(bf16)
    h2_ref[0] = hb
    hb_ref[...] = hb
    neg = jnp.full((SUBLANES, LANES), -jnp.inf, f32)

    def scores(h, slot):
        q = _dot(hb_ref[...], wq_ref[h])
        st = _dot_nt(kd_ref[h], q.astype(bf16))
        for p in range(2):
            for c in range(tt // LANES):
                st_ref[slot, p, c * STAGE_PITCH:c * STAGE_PITCH + nk, :] = (
                    st[p * nk:(p + 1) * nk, c * LANES:(c + 1) * LANES])

    scores(0, 0)

    def head(h, tie):
        slot = lax.rem(h, 2)
        for p in range(2):
            exact = _swap_by_value_then_index
            for grp in range(nk // k):
                items = [(st_ref[slot, p, pl.ds(grp * k + e, SUBLANES, stride=STAGE_PITCH), :],
                          jnp.full((SUBLANES, LANES), float(grp * k + e), f32)) for e in range(k)]
                for e, (v, idx) in enumerate(_run_net(items, _batcher_pairs(k), exact)):
                    lv_ref[grp, e] = v
                    li_ref[grp, e] = idx
            span = 1
            while span < nk // k:
                for ga in range(0, nk // k, 2 * span):
                    gb = ga + span
                    top = _merge_top([(lv_ref[ga, e], li_ref[ga, e]) for e in range(k)],
                                     [(lv_ref[gb, e], li_ref[gb, e]) for e in range(k)], exact)
                    for e, (v, idx) in enumerate(top):
                        if 2 * span < nk // k:
                            lv_ref[ga, e] = v
                            li_ref[ga, e] = idx
                        else:
                            ts_ref[p, e] = v
                            ti_ref[p, e] = idx
                span *= 2

        best, loser = _top_pairs(lambda a, b: (ts_ref[0, a] + ts_ref[1, b], ti_ref[0, a], ti_ref[1, b]), neg)
        tie = jnp.maximum(tie, (best[k - 1][0] == loser).astype(f32))
        es = [jnp.exp(v - best[0][0]) for v, _, _ in best]
        den = functools.reduce(lambda u, w: u + w, es)
        for e in range(k):
            r0 = pl.multiple_of((h * k + e) * SUBLANES, SUBLANES)
            oi_ref[pl.ds(r0, SUBLANES), :] = best[e][1]
            oj_ref[pl.ds(r0, SUBLANES), :] = best[e][2]
            og_ref[pl.ds(r0, SUBLANES), :] = es[e] / den
        scores(jnp.minimum(h + 1, PEER_HEADS - 1), 1 - slot)
        return tie

    tie = lax.fori_loop(0, PEER_HEADS, head, jnp.zeros((SUBLANES, LANES), f32))
    for c in range(tt // LANES):
        rows = slice(c * LANES, (c + 1) * LANES)
        i_ref[0, rows, :] = oi_ref[pl.ds(c, PEER_HK, stride=SUBLANES), :].T
        j_ref[0, rows, :] = oj_ref[pl.ds(c, PEER_HK, stride=SUBLANES), :].T
        gate_ref[0, rows, :] = og_ref[pl.ds(c, PEER_HK, stride=SUBLANES), :].T

    tie_ref[...] = tie
    rows_per_chunk = EXACT_CHUNK // LANES

    def chunk(ci, carry):
        flagged = jnp.max(tie_ref[pl.ds(ci * rows_per_chunk, rows_per_chunk), :]) > 0.0

        @pl.when(flagged)
        def _exact():
            r0 = pl.multiple_of(ci * EXACT_CHUNK, EXACT_CHUNK)
            hc = _rms_mod(x_ref[0, pl.ds(r0, EXACT_CHUNK), :], g_ref[...], sh_ref[0], sc_ref[0])
            ii, jj, gg = _route_exact_chunk(hc, wq_ref, kd_ref)
            i_ref[0, pl.ds(r0, EXACT_CHUNK), :] = ii
            j_ref[0, pl.ds(r0, EXACT_CHUNK), :] = jj
            gate_ref[0, pl.ds(r0, EXACT_CHUNK), :] = gg

        return carry

    lax.fori_loop(0, tt // EXACT_CHUNK, chunk, 0)


def _route(x, g, sh, sc, wq, kd):
    b, s, d = x.shape
    tt = ROUTE_TILE
    k = PEER_TOPK
    tok = lambda w: pl.BlockSpec((1, tt, w), lambda i, j: (i, j, 0))
    entry = (SUBLANES, LANES)
    return pl.pallas_call(
        functools.partial(_route_kernel, tt),
        grid=(b, s // tt),
        in_specs=[tok(d), _const_spec((1, d)),
                  pl.BlockSpec((1, 1, d), lambda i, j: (i, 0, 0)),
                  pl.BlockSpec((1, 1, d), lambda i, j: (i, 0, 0)),
                  _const_spec(wq.shape), _const_spec(kd.shape)],
        out_specs=[tok(d), tok(PEER_HK), tok(PEER_HK), tok(PEER_HK)],
        out_shape=[jax.ShapeDtypeStruct((b, s, d), bf16)] + [jax.ShapeDtypeStruct((b, s, PEER_HK), f32)] * 3,
        scratch_shapes=[pltpu.VMEM((tt, d), bf16),
                        pltpu.VMEM((2, 2, (tt // LANES) * STAGE_PITCH, LANES), f32),
                        pltpu.VMEM((PEER_NKEYS // k, k) + entry, f32), pltpu.VMEM((PEER_NKEYS // k, k) + entry, f32),
                        pltpu.VMEM((2, k) + entry, f32), pltpu.VMEM((2, k) + entry, f32),
                        pltpu.VMEM((PEER_HK * SUBLANES, LANES), f32), pltpu.VMEM((PEER_HK * SUBLANES, LANES), f32),
                        pltpu.VMEM((PEER_HK * SUBLANES, LANES), f32), pltpu.VMEM(entry, f32)],
        compiler_params=_params(("parallel", "parallel"), 56),
        name="peer_route",
    )(x, g.reshape(1, d), sh, sc, wq, kd)


EXPERT_BLOCK = 2048
I_PER_BLOCK = EXPERT_BLOCK // PEER_NKEYS
N_EXPERT_BLOCKS = PEER_NKEYS * PEER_NKEYS // EXPERT_BLOCK
G_PITCH_PAD = SUBLANES
BUILD_PAIRS = 16
ROW_SPLITS = 2


def _two_gelu(a):
    return a * (1.0 + lax.erf(a * (2.0 ** -0.5)))


def _bf16_bits(a):
    return lax.bitcast_convert_type(a.astype(bf16).astype(f32), jnp.uint32)


def _expert_kernel(tt, final, x_ref, h2_ref, i_ref, j_ref, gate_ref, gt_ref, ut_ref, v_ref, fg_ref, o_ref, g_buf, acc_ref):
    ib = pl.program_id(2)
    pitch = tt // 2 + G_PITCH_PAD

    @pl.when(ib == 0)
    def _build():
        sub = lax.broadcasted_iota(jnp.int32, (PEER_NKEYS, PEER_HK), 0).astype(f32).astype(bf16)
        one = jnp.ones((), bf16)
        zero = jnp.zeros((), bf16)

        def token_grid(t):
            irow = i_ref[0, pl.ds(t, 1), :].astype(bf16)
            jrow = j_ref[0, pl.ds(t, 1), :].astype(bf16)
            grow = (0.5 * gate_ref[0, pl.ds(t, 1), :]).astype(bf16)
            pt = jnp.where(sub == irow, one, zero)
            qt = jnp.where(sub == jrow, grow, zero)
            return _bf16_bits(_dot_nt(pt, qt))

        def body(step, carry):
            for u in range(BUILD_PAIRS):
                w = step * BUILD_PAIRS + u
                packed = lax.bitwise_or(token_grid(2 * w + 1), lax.shift_right_logical(token_grid(2 * w), jnp.uint32(16)))
                g_buf[pl.ds(w, PEER_NKEYS, stride=pitch), :] = packed
            return carry

        lax.fori_loop(0, tt // (2 * BUILD_PAIRS), body, 0)
        acc_ref[...] = jnp.zeros_like(acc_ref)

    rows = tt // ROW_SPLITS
    for r in range(0, tt, rows):
        a = _dot(h2_ref[0, r:r + rows, :], ut_ref[...]).astype(bf16)
        pieces = []
        for k in range(I_PER_BLOCK):
            row0 = pl.multiple_of((ib * I_PER_BLOCK + k) * pitch + r // 2, SUBLANES)
            gk = pltpu.bitcast(g_buf[pl.ds(row0, rows // 2), :], bf16)
            pieces.append(gk * _two_gelu(a[:, k * PEER_NKEYS:(k + 1) * PEER_NKEYS]))
        acc_ref[r:r + rows, :] += _dot(jnp.concatenate(pieces, axis=1), v_ref[...])

    @pl.when(ib == N_EXPERT_BLOCKS - 1)
    def _finish():
        y = x_ref[0] + gt_ref[0] * acc_ref[...]
        if final:
            ms = jnp.mean(y * y, axis=-1, keepdims=True)
            y = y * lax.rsqrt(ms + NORM_EPS) * fg_ref[...]
        o_ref[0] = y


def _experts(x, h2, idx_i, idx_j, gate, gt, ut, v, layer, final_g, final, tt):
    b, s, d = x.shape
    tok = lambda w: pl.BlockSpec((1, tt, w), lambda i, j, e: (i, j, 0))
    return pl.pallas_call(
        functools.partial(_expert_kernel, tt, final),
        grid=(b, s // tt, N_EXPERT_BLOCKS),
        in_specs=[tok(d), tok(d), tok(PEER_HK), tok(PEER_HK), tok(PEER_HK),
                  pl.BlockSpec((1, 1, d), lambda i, j, e: (i, 0, 0)),
                  pl.BlockSpec((None, d, EXPERT_BLOCK), lambda i, j, e: (layer, 0, e)),
                  pl.BlockSpec((None, EXPERT_BLOCK, d), lambda i, j, e: (layer, e, 0)),
                  pl.BlockSpec((1, d), lambda i, j, e: (0, 0))],
        out_specs=tok(d),
        out_shape=jax.ShapeDtypeStruct((b, s, d), f32),
        scratch_shapes=[pltpu.VMEM((PEER_NKEYS * (tt // 2 + G_PITCH_PAD), PEER_NKEYS), jnp.uint32),
                        pltpu.VMEM((tt, d), f32)],
        compiler_params=_params(("parallel", "parallel", "arbitrary"), 56),
        name="peer_experts",
    )(x, h2, idx_i, idx_j, gate, gt, ut, v, final_g.reshape(1, d))


def _rope_tables(seq):
    t = jnp.arange(seq)
    pos = jnp.stack([t // GRID_W, t % GRID_W], axis=1).astype(f32)
    f = HEAD_DIM // 4
    inv = ROPE_BASE ** (-jnp.arange(f, dtype=f32) / f)
    lane = jnp.arange(LANES)
    l64 = lane % HEAD_DIM
    ang = pos[:, (l64 // (2 * f))] * inv[l64 % f][None, :]
    first = (l64 % (2 * f)) < f
    cos = jnp.cos(ang)
    sin = jnp.sin(ang)
    return cos, jnp.where(first[None, :], -sin, 0.0), jnp.where(first[None, :], 0.0, sin)


def _na_bias_table(rpb, rows):
    r0 = np.array(_na_class_rows(rows))[:, None, None]
    a = np.arange(NA_ROWS)[None, :, None]
    kr = np.arange(NA_SPAN)[None, None, :]
    r = r0 + a
    ws = np.clip(r - NA_KH // 2, 0, rows - NA_KH)
    absrow = np.clip(r0 - NA_KH // 2, 0, rows - NA_SPAN) + kr
    ok_r = (absrow >= ws) & (absrow < ws + NA_KH)
    dr = absrow - r + NA_KH - 1
    oh_r = (ok_r[..., None] & (dr[..., None] == np.arange(2 * NA_KH - 1))).astype(np.float32)
    sel = jnp.einsum("cafd,hde->hcafe", jnp.asarray(oh_r), rpb.astype(f32), precision=lax.Precision.HIGHEST)
    sel = jnp.where(jnp.asarray(ok_r)[None, :, :, :, None], sel, NEG_BIG)
    sel = sel.reshape(NA_HEADS, NA_CLASSES, NA_ROWS * NA_SPAN, 2 * NA_KW - 1)
    sel = jnp.pad(sel, ((0, 0), (0, 0), (0, BIAS_SEL_ROWS - NA_ROWS * NA_SPAN), (0, LANES - (2 * NA_KW - 1))),
                  constant_values=NEG_BIG)
    qc = np.arange(GRID_W)[:, None]
    kc = np.arange(GRID_W)[None, :]
    cstart = np.clip(qc - NA_KW // 2, 0, GRID_W - NA_KW)
    ok_c = (kc >= cstart) & (kc < cstart + NA_KW)
    cmask = np.where(np.concatenate([ok_c, ok_c], axis=1), 0.0, NEG_BIG).astype(np.float32)
    return pl.pallas_call(
        _na_bias_kernel,
        grid=(NA_HEADS, NA_CLASSES),
        in_specs=[pl.BlockSpec((1, 1, BIAS_SEL_ROWS, LANES), lambda h, c: (h, c, 0, 0)),
                  _const_spec(cmask.shape)],
        out_specs=pl.BlockSpec((1, 1, NA_ROWS * GRID_W, NA_SPAN * GRID_W), lambda h, c: (h, c, 0, 0)),
        out_shape=jax.ShapeDtypeStruct((NA_HEADS, NA_CLASSES, NA_ROWS * GRID_W, NA_SPAN * GRID_W), f32),
        compiler_params=_params(("parallel", "parallel"), 16),
        name="nbr_bias",
    )(sel, jnp.asarray(cmask))


BIAS_SEL_ROWS = 24


def _na_bias_kernel(sel_ref, cmask_ref, o_ref):
    lane = lax.broadcasted_iota(jnp.int32, (GRID_W, LANES), 1)
    first = lane < GRID_W
    cm = cmask_ref[...]
    for a in range(NA_ROWS):
        rows = slice(a * GRID_W, (a + 1) * GRID_W)
        for f0 in range(0, NA_SPAN, 2):
            def tile(f, lane0):
                v = jnp.broadcast_to(sel_ref[0, 0, a * NA_SPAN + f:a * NA_SPAN + f + 1, :], (GRID_W, LANES))
                return pltpu.roll(v, (lane0 - (NA_KW - 1)) % LANES, 1, stride=1, stride_axis=0)
            if f0 + 1 < NA_SPAN:
                t = jnp.where(first, tile(f0, 0), tile(f0 + 1, GRID_W)) + cm
                o_ref[0, 0, rows, f0 * GRID_W:(f0 + 2) * GRID_W] = t
            else:
                t = tile(f0, 0) + cm
                o_ref[0, 0, rows, f0 * GRID_W:(f0 + 1) * GRID_W] = t[:, :GRID_W]


def _layout_w_in(w):
    c = lambda off, width: w[..., off:off + width]
    hd = HEAD_DIM
    k0, k1 = c(OFF_KW, hd), c(OFF_KW + hd, hd)
    v0, v1 = c(OFF_VW, hd), c(OFF_VW + hd, hd)
    cols = [c(OFF_KN, NA_WIDTH), c(OFF_VN, NA_WIDTH), k0, k0, k1, k1, v0, v0, v1, v1,
            c(OFF_QN, NA_WIDTH), c(OFF_QW, SWA_Q_WIDTH), c(OFF_A, POOL_WIDTH), c(OFF_G, 3 * D_MODEL)]
    return jnp.concatenate(cols, axis=-1).astype(bf16)


def _layout_peer_queries(wq):
    d = wq.shape[0]
    return wq.reshape(d, PEER_HEADS, 2 * PEER_NKEYS).transpose(1, 0, 2)


def _layout_peer_keys(keys):
    h, _, nk, dk = keys.shape
    z = jnp.zeros((h, nk, dk), keys.dtype)
    return jnp.concatenate([jnp.concatenate([keys[:, 0], z], axis=2), jnp.concatenate([z, keys[:, 1]], axis=2)], axis=1)


LATENT_TILE = 512
EXPERT_TILE = 512


def kernel(x, c, ctx, c_ctx, ada_w, ada_b, norm1_g, norm2_g, w_in, pool_w, pool_scale, na_rpb, swa_sink,
           w_branch_a, w_branch_b, w_branch_c, w_out, peer_wq, peer_keys, peer_u, peer_v, final_g):
    b, s, d = x.shape
    lc = ctx.shape[1]
    depth = ada_w.shape[0]
    cpad = jnp.concatenate([c, c_ctx[None, :], jnp.zeros((SUBLANES - b - 1, d), f32)], axis=0)
    mod = _modulation(cpad, ada_w, ada_b)
    rope_tabs = _rope_tables(s)
    w_aug = _layout_w_in(w_in)
    ut = jnp.swapaxes(peer_u, 1, 2).astype(bf16)
    vb = peer_v.astype(bf16)
    xc = ctx
    for l in range(depth):
        last = l == depth - 1
        ml = mod[l, :b].reshape(b, 6, 1, d)
        sh1, sc1, gt1, sh2, sc2, gt2 = (ml[:, i] for i in range(6))
        mc = jnp.broadcast_to(mod[l, b].reshape(1, 6, 1, d), (b, 6, 1, d))
        csh1, csc1, cgt1, csh2, csc2, cgt2 = (mc[:, i] for i in range(6))

        wa, wb, wc, wo = (w[l].astype(bf16) for w in (w_branch_a, w_branch_b, w_branch_c, w_out))
        pw = pool_w[l].astype(bf16)
        bias_tbl = _na_bias_table(na_rpb[l], s // GRID_W)
        wq = _layout_peer_queries(peer_wq[l]).astype(bf16)
        kd = _layout_peer_keys(peer_keys[l]).astype(bf16)

        knvn, sw, q, a, g = _inproj(x, norm1_g[l], sh1, sc1, w_aug, l, SEGS_ALL, rope_tabs, LATENT_TILE)
        if last:
            knvn_c, sw_c = _inproj(xc, norm1_g[l], csh1, csc1, w_aug, l, SEGS_KV, None, lc)
        else:
            knvn_c, sw_c, q_c, a_c, g_c = _inproj(xc, norm1_g[l], csh1, csc1, w_aug, l, SEGS_ALL, None, lc)
        ya = _pool(a, pw, pool_scale[l], 256)
        ob = _na(q, knvn, knvn_c, bias_tbl)
        oc = _swa(q, sw, sw_c, swa_sink[l])
        x = _merge(x, ya, ob, oc, g, gt1, wa, wb, wc, wo, LATENT_TILE)
        if not last:
            ya_c = _pool(a_c, pw, pool_scale[l], lc)
            ob_c, oc_c = _ctx_attn(q_c, knvn_c, sw_c, swa_sink[l])
            xc = _merge(xc, ya_c, ob_c, oc_c, g_c, cgt1, wa, wb, wc, wo, lc)

        h2, ii, jj, gate = _route(x, norm2_g[l], sh2, sc2, wq, kd)
        x = _experts(x, h2, ii, jj, gate, gt2, ut, vb, l, final_g, last, EXPERT_TILE)
        if not last:
            xf = xc.reshape(1, b * lc, d)
            h2c, iic, jjc, gatec = _route(xf, norm2_g[l], csh2[:1], csc2[:1], wq, kd)
            xc = _experts(xf, h2c, iic, jjc, gatec, cgt2[:1], ut, vb, l, final_g, False, EXPERT_TILE).reshape(b, lc, d)
    return x
```
